```python
import jax, jax.numpy as jnp
from jax import lax
import numpy as np

D_MODEL = 2048
BATCH = 4
SEQ = 2048
DEPTH = 2
DEC_BATCH = 128
DEC_SEQ = 8
PAST_LEN = 16384
PAGE_SIZE = 128

POOL_WIDTH = D_MODEL // 2
POOL_WINDOWS = (2, 4, 8, 16)
POOL_GROUPS = len(POOL_WINDOWS)
POOL_GROUP_DIM = POOL_WIDTH // POOL_GROUPS
POOL_BUF = max(POOL_WINDOWS) - 1
LRU_WIDTH = D_MODEL
LRU_BLOCKS = 16
LRU_BLOCK_DIM = LRU_WIDTH // LRU_BLOCKS
CONV_WIDTH = 4
CONV_BUF = CONV_WIDTH - 1
LRU_C = 8.0
IN_COLS = 2 * POOL_WIDTH + 2 * LRU_WIDTH + 2 * D_MODEL
DEEPNORM_ALPHA = (2.0 * DEPTH) ** 0.25
DEEPNORM_BETA = (8.0 * DEPTH) ** -0.25
LN_EPS = 1e-5

kernel_name = "hybrid_pool_rglru_gated_merge_step"


def _layernorm(z, g, b):
    z = z.astype(jnp.float32)
    mu = jnp.mean(z, axis=-1, keepdims=True)
    var = jnp.mean(jnp.square(z - mu), axis=-1, keepdims=True)
    return (z - mu) * lax.rsqrt(var + LN_EPS) * g + b


def _lin_combine(left, right):
    a1, b1 = left
    a2, b2 = right
    return a1 * a2, a2 * b1 + b2


def _layer(x, start, pool_buf, conv_buf, h0, w_in, b_merge, pool_w, pool_scale,
           conv_w, conv_b, lru_wa, lru_ba, lru_wx, lru_bx, lru_L,
           w_proj_a, w_proj_b, w_out, ln_g, ln_b):
    Bn, T, _ = x.shape
    f32 = jnp.float32
    proj = jnp.einsum('btd,de->bte', x, w_in)
    cuts = [POOL_WIDTH, 2 * POOL_WIDTH, 2 * POOL_WIDTH + LRU_WIDTH,
            2 * POOL_WIDTH + 2 * LRU_WIDTH, 2 * POOL_WIDTH + 2 * LRU_WIDTH + D_MODEL]
    u_a, gate_a, u_b, gate_b, m_a, m_b = jnp.split(proj, cuts, axis=-1)

    u_ext = jnp.concatenate([pool_buf.astype(u_a.dtype), u_a], axis=1)
    cs = jnp.cumsum(u_ext.astype(f32), axis=1)
    cs = jnp.pad(cs, ((0, 0), (1, 0), (0, 0)))
    pos = start + jnp.arange(T, dtype=jnp.int32)
    pooled_groups = []
    for g, w in enumerate(POOL_WINDOWS):
        lo, hi = g * POOL_GROUP_DIM, (g + 1) * POOL_GROUP_DIM
        s = (cs[:, POOL_BUF + 1:POOL_BUF + 1 + T, lo:hi]
             - cs[:, POOL_BUF + 1 - w:POOL_BUF + 1 - w + T, lo:hi])
        cnt = jnp.minimum(pos + 1, w).astype(f32)[None, :, None]
        pooled_groups.append(s / cnt)
    pooled = jnp.stack(pooled_groups, axis=2)
    diff = pooled - u_a.astype(f32).reshape(Bn, T, POOL_GROUPS, POOL_GROUP_DIM)
    mixed_a = jnp.einsum('btgc,gcd->btgd', diff, pool_w).reshape(Bn, T, POOL_WIDTH) * pool_scale
    y_a = mixed_a * jax.nn.silu(gate_a)

    xb_ext = jnp.concatenate([conv_buf.astype(u_b.dtype), u_b], axis=1)
    xc = conv_b + xb_ext[:, 0:T] * conv_w[0]
    for k in range(1, CONV_WIDTH):
        xc = xc + xb_ext[:, k:k + T] * conv_w[k]
    xcb = xc.reshape(Bn, T, LRU_BLOCKS, LRU_BLOCK_DIM)
    r = jax.nn.sigmoid(jnp.einsum('btnc,ncd->btnd', xcb, lru_wa).reshape(Bn, T, LRU_WIDTH) + lru_ba)
    i = jax.nn.sigmoid(jnp.einsum('btnc,ncd->btnd', xcb, lru_wx).reshape(Bn, T, LRU_WIDTH) + lru_bx)
    log_a = (-LRU_C * r.astype(f32)) * jax.nn.softplus(-lru_L.astype(f32))
    a = jnp.exp(log_a)
    b = jnp.sqrt(-jnp.expm1(2.0 * log_a)) * (i * xc).astype(f32)
    b = b.at[:, 0].add(a[:, 0] * h0.astype(f32))
    _, h = lax.associative_scan(_lin_combine, (a, b), axis=1)
    y_b = h * jax.nn.silu(gate_b)

    g_a = jax.nn.sigmoid(m_a + b_merge[0])
    g_b = jax.nn.sigmoid(m_b + b_merge[1])
    merged = (g_a * jnp.einsum('btc,cd->btd', y_a, w_proj_a)
              + g_b * jnp.einsum('btc,cd->btd', y_b, w_proj_b))
    out = jnp.einsum('btd,de->bte', merged, w_out)
    y = _layernorm(DEEPNORM_ALPHA * x.astype(f32) + out.astype(f32), ln_g, ln_b)
    return (y.astype(x.dtype), u_ext[:, -POOL_BUF:], xb_ext[:, -CONV_BUF:], h[:, -1])


def setup_inputs(seed: int = 0) -> dict:
    key = jax.random.key(seed)
    ks = jax.random.split(key, 24)
    nrm = lambda k, shape, s: jax.random.normal(k, shape, jnp.float32) * s
    a_base = jax.random.uniform(ks[14], (DEPTH, LRU_WIDTH), jnp.float32, 0.9, 0.999)
    sig = a_base ** (1.0 / LRU_C)
    lru_L = jnp.log(sig) - jnp.log1p(-sig)
    return {
        "x_prompt": nrm(ks[0], (BATCH, SEQ, D_MODEL), 1.0),
        "x_sample": nrm(ks[1], (DEC_BATCH, DEC_SEQ, D_MODEL), 1.0),
        "state_pool": nrm(ks[2], (DEPTH, DEC_BATCH, POOL_BUF, POOL_WIDTH), 1.0),
        "state_conv": nrm(ks[3], (DEPTH, DEC_BATCH, CONV_BUF, LRU_WIDTH), 1.0),
        "state_h": nrm(ks[4], (DEPTH, DEC_BATCH, LRU_WIDTH), 0.5),
        "w_in": nrm(ks[5], (DEPTH, D_MODEL, IN_COLS), D_MODEL ** -0.5),
        "b_merge": nrm(ks[6], (DEPTH, 2, D_MODEL), 0.02),
        "pool_w": nrm(ks[7], (DEPTH, POOL_GROUPS, POOL_GROUP_DIM, POOL_GROUP_DIM), POOL_GROUP_DIM ** -0.5),
        "pool_scale": 1.0 + nrm(ks[8], (DEPTH, POOL_WIDTH), 0.1),
        "conv_w": nrm(ks[9], (DEPTH, CONV_WIDTH, LRU_WIDTH), CONV_WIDTH ** -0.5),
        "conv_b": nrm(ks[10], (DEPTH, LRU_WIDTH), 0.02),
        "lru_wa": nrm(ks[11], (DEPTH, LRU_BLOCKS, LRU_BLOCK_DIM, LRU_BLOCK_DIM), LRU_BLOCK_DIM ** -0.5),
        "lru_ba": nrm(ks[12], (DEPTH, LRU_WIDTH), 0.02),
        "lru_wx": nrm(ks[13], (DEPTH, LRU_BLOCKS, LRU_BLOCK_DIM, LRU_BLOCK_DIM), LRU_BLOCK_DIM ** -0.5),
        "lru_bx": nrm(ks[15], (DEPTH, LRU_WIDTH), 0.02),
        "lru_L": lru_L,
        "w_proj_a": nrm(ks[16], (DEPTH, POOL_WIDTH, D_MODEL), DEEPNORM_BETA * POOL_WIDTH ** -0.5),
        "w_proj_b": nrm(ks[17], (DEPTH, LRU_WIDTH, D_MODEL), DEEPNORM_BETA * LRU_WIDTH ** -0.5),
        "w_out": nrm(ks[18], (DEPTH, D_MODEL, D_MODEL), DEEPNORM_BETA * D_MODEL ** -0.5),
        "ln_g": 1.0 + nrm(ks[19], (DEPTH, D_MODEL), 0.02),
        "ln_b": nrm(ks[20], (DEPTH, D_MODEL), 0.02),
    }


def reference(x_prompt, x_sample, state_pool, state_conv, state_h,
              w_in, b_merge, pool_w, pool_scale, conv_w, conv_b,
              lru_wa, lru_ba, lru_wx, lru_bx, lru_L,
              w_proj_a, w_proj_b, w_out, ln_g, ln_b):
    Bp = x_prompt.shape[0]
    yp, ys = x_prompt, x_sample
    pool_p, conv_p, h_p, pool_s, conv_s, h_s = [], [], [], [], [], []
    zero_pool = jnp.zeros((Bp, POOL_BUF, POOL_WIDTH), x_prompt.dtype)
    zero_conv = jnp.zeros((Bp, CONV_BUF, LRU_WIDTH), x_prompt.dtype)
    zero_h = jnp.zeros((Bp, LRU_WIDTH), jnp.float32)
    for l in range(DEPTH):
        params = (w_in[l], b_merge[l], pool_w[l], pool_scale[l], conv_w[l], conv_b[l],
                  lru_wa[l], lru_ba[l], lru_wx[l], lru_bx[l], lru_L[l],
                  w_proj_a[l], w_proj_b[l], w_out[l], ln_g[l], ln_b[l])
        yp, pb, cb, hl = _layer(yp, 0, zero_pool, zero_conv, zero_h, *params)
        pool_p.append(pb); conv_p.append(cb); h_p.append(hl)
        ys, pb, cb, hl = _layer(ys, PAST_LEN, state_pool[l], state_conv[l], state_h[l], *params)
        pool_s.append(pb); conv_s.append(cb); h_s.append(hl)
    return (yp, ys,
            jnp.stack(pool_p), jnp.stack(conv_p), jnp.stack(h_p),
            jnp.stack(pool_s), jnp.stack(conv_s), jnp.stack(h_s))
```

```python
import functools

import jax
import jax.numpy as jnp
from jax import lax
from jax.experimental import pallas as pl
from jax.experimental.pallas import tpu as pltpu

D_MODEL = 2048
POOL_WIDTH = D_MODEL // 2
POOL_WINDOWS = (2, 4, 8, 16)
POOL_BUF = max(POOL_WINDOWS) - 1
LRU_WIDTH = D_MODEL
LRU_BLOCK_DIM = 128
CONV_WIDTH = 4
CONV_BUF = CONV_WIDTH - 1
LRU_C = 8.0
LN_EPS = 1e-5

CB = 256
N_PB = POOL_WIDTH // CB
N_LB = LRU_WIDTH // CB
SEG = 8
PROMPT_STEPS = 64
SAMPLE_ROWS = 32
MERGE_ROWS = 256
MERGE_COLS = 512
VMEM_LIMIT_BYTES = 60 * 1024 * 1024

F32 = jnp.float32
BF16 = jnp.bfloat16


def _silu(v):
    return v * jax.nn.sigmoid(v)


def _dot(a, b):
    return jnp.dot(a, b, preferred_element_type=F32)


def _inv_count(rows, R, S, window, pos0, chained):
    assert R & (R - 1) == 0
    i = lax.broadcasted_iota(jnp.int32, (rows, 1), 0)
    step, seg = lax.shift_right_logical(i, R.bit_length() - 1), lax.bitwise_and(i, R - 1)
    pos = pos0 + (seg * S + step if chained else step)
    return 1.0 / jnp.minimum(pos + 1, window).astype(F32)


def _pool_block(g, R, S, xb_s, ep_s, wua, wga, poolw, pscale, inv_cnt, fill_halo, ya_ref):
    TM = S * R
    H = POOL_BUF * R
    w = POOL_WINDOWS[g]
    xb = xb_s[...]
    u = _dot(xb, wua[g])
    fill_halo(u)
    ep_s[pl.ds(H, TM), :] = u
    acc = u
    for k in range(1, w):
        acc = acc + ep_s[pl.ds(H - k * R, TM), :]
    diff = acc * inv_cnt - u
    mixed = _dot(diff.astype(BF16), poolw[g]) * pscale[g]
    gate = _dot(xb, wga[g])
    ya_ref[g] = (mixed * _silu(gate)).astype(BF16)
    return u


def _lru_pre(blk, R, S, xb_s, ec_s, wub, convw, convb, wabd, wxbd, ba, bx, lL, fill_halo, a_s, b_s):
    TM = S * R
    H = CONV_BUF * R
    xb = xb_s[...]
    u = _dot(xb, wub[blk])
    fill_halo(u)
    ec_s[pl.ds(H, TM), :] = u
    cw = convw[blk]
    xc = convb[blk] + ec_s[pl.ds(0, TM), :] * cw[0:1, :]
    for k in range(1, CONV_WIDTH):
        xc = xc + ec_s[pl.ds(k * R, TM), :] * cw[k:k + 1, :]
    xcb = xc.astype(BF16)
    r = jax.nn.sigmoid(_dot(xcb, wabd[blk]) + ba[blk])
    i = jax.nn.sigmoid(_dot(xcb, wxbd[blk]) + bx[blk])
    nsp = -LRU_C * jax.nn.softplus(-lL[blk])
    log_a = r * nsp
    a = jnp.exp(log_a)
    a_s[...] = a
    b_s[...] = jnp.sqrt((1.0 - a) * (1.0 + a)) * (i * xc)
    return u


def _halo_from_tail(tail, carry, n):
    t3 = tail.reshape(n, SEG, CB)
    sub = lax.broadcasted_iota(jnp.int32, (n, SEG, CB), 1)
    return jnp.where(sub == 0, carry.reshape(n, SEG, CB), pltpu.roll(t3, 1, 1)).reshape(n * SEG, CB)


def _last_seg_rows(tail, n):
    return tail.reshape(n, SEG, CB)[:, SEG - 1, :]


def _bcast_last_seg(tail, n):
    t3 = tail.reshape(n, SEG, CB)
    return jnp.broadcast_to(t3[:, SEG - 1:SEG, :], (n, SEG, CB)).reshape(n * SEG, CB)


def _mixer_prompt_kernel(S, x_ref, wua, wga, wub, wgb, poolw, pscale, convw, convb,
                         wabd, wxbd, ba, bx, lL,
                         ya_ref, yb_ref, pst_ref, cst_ref, hst_ref,
                         xb_s, ep_s, pc_s, ec_s, cc_s, hc_s, a_s, b_s):
    R = SEG
    TM = S * R
    t = pl.program_id(1)

    @pl.when(t == 0)
    def _():
        pc_s[...] = jnp.zeros_like(pc_s)
        cc_s[...] = jnp.zeros_like(cc_s)
        hc_s[...] = jnp.zeros_like(hc_s)

    xb_s[...] = x_ref[0, 0].astype(BF16)

    for g in range(N_PB):
        inv_cnt = _inv_count(TM, R, S, POOL_WINDOWS[g], t * TM, True)

        def fill_pool_halo(u, g=g):
            tail = u[TM - POOL_BUF * R:, :]
            ep_s[pl.ds(0, POOL_BUF * R), :] = _halo_from_tail(tail, pc_s[g], POOL_BUF)
            pc_s[g] = _bcast_last_seg(tail, POOL_BUF)
            pst_ref[0, g] = _last_seg_rows(tail, POOL_BUF)

        _pool_block(g, R, S, xb_s, ep_s, wua, wga, poolw, pscale, inv_cnt, fill_pool_halo, ya_ref)

    sub = lax.broadcasted_iota(jnp.int32, (R, CB), 0)

    def lru_block(blk, carry):
        def fill_conv_halo(u):
            tail = u[TM - CONV_BUF * R:, :]
            ec_s[pl.ds(0, CONV_BUF * R), :] = _halo_from_tail(tail, cc_s[blk], CONV_BUF)
            cc_s[blk] = _bcast_last_seg(tail, CONV_BUF)
            cst_ref[0, blk] = _last_seg_rows(tail, CONV_BUF)

        _lru_pre(blk, R, S, xb_s, ec_s, wub, convw, convb, wabd, wxbd, ba, bx, lL,
                 fill_conv_halo, a_s, b_s)

        def step(j, hp):
            h, p = hp
            rows = pl.ds(pl.multiple_of(j * R, R), R)
            a = a_s[rows, :]
            h = a * h + b_s[rows, :]
            p = a * p
            b_s[rows, :] = h
            a_s[rows, :] = p
            return h, p

        h_end, p_end = lax.fori_loop(0, S, step, (jnp.zeros((R, CB), F32), jnp.ones((R, CB), F32)),
                                     unroll=8)

        cin = hc_s[blk]
        c = cin
        for s in range(1, R + 1):
            cin = (jnp.broadcast_to(p_end[s - 1:s, :], (R, CB)) * cin
                   + jnp.broadcast_to(h_end[s - 1:s, :], (R, CB)))
            if s < R:
                c = jnp.where(sub == s, cin, c)
        hc_s[blk] = cin
        hst_ref[0, blk] = cin[0:1, :]

        h = b_s[...].reshape(S, R, CB) + a_s[...].reshape(S, R, CB) * c[None]
        gate = _dot(xb_s[...], wgb[blk])
        yb_ref[blk] = (h.reshape(TM, CB) * _silu(gate)).astype(BF16)
        return carry

    lax.fori_loop(0, N_LB, lru_block, 0)


def _mixer_sample_kernel(S, R, pos0, x_ref, pin_ref, cin_ref, hin_ref,
                         wua, wga, wub, wgb, poolw, pscale, convw, convb,
                         wabd, wxbd, ba, bx, lL,
                         ya_ref, yb_ref, pst_ref, cst_ref, hst_ref,
                         xb_s, ep_s, ec_s, a_s, b_s):
    TM = S * R
    xb_s[...] = x_ref[0].astype(BF16)

    for g in range(N_PB):
        inv_cnt = _inv_count(TM, R, S, POOL_WINDOWS[g], pos0, False)

        def fill_pool_halo(u, g=g):
            ep_s[pl.ds(0, POOL_BUF * R), :] = pin_ref[0, g]

        _pool_block(g, R, S, xb_s, ep_s, wua, wga, poolw, pscale, inv_cnt, fill_pool_halo, ya_ref)
        pst_ref[0, g] = ep_s[pl.ds(S * R, POOL_BUF * R), :]

    def lru_block(blk, carry):
        def fill_conv_halo(u):
            ec_s[pl.ds(0, CONV_BUF * R), :] = cin_ref[0, blk]

        _lru_pre(blk, R, S, xb_s, ec_s, wub, convw, convb, wabd, wxbd, ba, bx, lL,
                 fill_conv_halo, a_s, b_s)
        cst_ref[0, blk] = ec_s[pl.ds(S * R, CONV_BUF * R), :]

        h = hin_ref[0, blk]
        for j in range(S):
            rows = pl.ds(j * R, R)
            h = a_s[rows, :] * h + b_s[rows, :]
            b_s[rows, :] = h
        hst_ref[0, blk] = h

        gate = _dot(xb_s[...], wgb[blk])
        yb_ref[blk] = (b_s[...] * _silu(gate)).astype(BF16)
        return carry

    lax.fori_loop(0, N_LB, lru_block, 0)


def _merge_kernel(alpha, x_ref, ya_ref, yb_ref, wma, wmb, bm, wpa, wpb, wout, lng, lnb,
                  o_ref, m_s):
    xf = x_ref[...]
    xb = xf.astype(BF16)
    for j in range(D_MODEL // MERGE_COLS):
        cs = slice(j * MERGE_COLS, (j + 1) * MERGE_COLS)
        pa = _dot(ya_ref[0], wpa[0, :, cs])
        for g in range(1, N_PB):
            pa = pa + _dot(ya_ref[g], wpa[g, :, cs])
        pb = _dot(yb_ref[0], wpb[0, :, cs])
        for g in range(1, N_LB):
            pb = pb + _dot(yb_ref[g], wpb[g, :, cs])
        g_a = jax.nn.sigmoid(_dot(xb, wma[:, cs]) + bm[0:1, cs])
        g_b = jax.nn.sigmoid(_dot(xb, wmb[:, cs]) + bm[1:2, cs])
        m_s[:, cs] = (g_a * pa + g_b * pb).astype(BF16)
    out = _dot(m_s[...], wout[...])
    z = alpha * xf + out
    mu = jnp.mean(z, axis=-1, keepdims=True)
    zc = z - mu
    var = jnp.mean(zc * zc, axis=-1, keepdims=True)
    o_ref[...] = zc * lax.rsqrt(var + LN_EPS) * lng[...] + lnb[...]


def _vmem_spec():
    return pl.BlockSpec(memory_space=pltpu.VMEM)


def _mixer_weight_specs():
    return [_vmem_spec() for _ in range(13)]


def _mixer_prompt(xp, mw):
    B, nT, TM, D = xp.shape
    S = TM // SEG
    N = B * nT * TM
    out_shape = (
        jax.ShapeDtypeStruct((N_PB, N, CB), BF16),
        jax.ShapeDtypeStruct((N_LB, N, CB), BF16),
        jax.ShapeDtypeStruct((B, N_PB, POOL_BUF, CB), F32),
        jax.ShapeDtypeStruct((B, N_LB, CONV_BUF, CB), F32),
        jax.ShapeDtypeStruct((B, N_LB, 1, CB), F32),
    )
    out_specs = (
        pl.BlockSpec((N_PB, TM, CB), lambda b, t: (0, b * nT + t, 0)),
        pl.BlockSpec((N_LB, TM, CB), lambda b, t: (0, b * nT + t, 0)),
        pl.BlockSpec((1, N_PB, POOL_BUF, CB), lambda b, t: (b, 0, 0, 0)),
        pl.BlockSpec((1, N_LB, CONV_BUF, CB), lambda b, t: (b, 0, 0, 0)),
        pl.BlockSpec((1, N_LB, 1, CB), lambda b, t: (b, 0, 0, 0)),
    )
    scratch = [
        pltpu.VMEM((TM, D), BF16),
        pltpu.VMEM(((POOL_BUF + S) * SEG, CB), F32),
        pltpu.VMEM((N_PB, POOL_BUF * SEG, CB), F32),
        pltpu.VMEM(((CONV_BUF + S) * SEG, CB), F32),
        pltpu.VMEM((N_LB, CONV_BUF * SEG, CB), F32),
        pltpu.VMEM((N_LB, SEG, CB), F32),
        pltpu.VMEM((TM, CB), F32),
        pltpu.VMEM((TM, CB), F32),
    ]
    return pl.pallas_call(
        functools.partial(_mixer_prompt_kernel, S),
        grid=(B, nT),
        in_specs=[pl.BlockSpec((1, 1, TM, D), lambda b, t: (b, t, 0, 0))] + _mixer_weight_specs(),
        out_specs=out_specs,
        out_shape=out_shape,
        scratch_shapes=scratch,
        compiler_params=pltpu.CompilerParams(
            dimension_semantics=("arbitrary", "arbitrary"), vmem_limit_bytes=VMEM_LIMIT_BYTES),
        name="mixer_prompt",
    )(xp, *mw)


def _mixer_sample(xs, pin, cin, hin, mw, S, pos0):
    nTs, TM, D = xs.shape
    R = TM // S
    N = nTs * TM
    out_shape = (
        jax.ShapeDtypeStruct((N_PB, N, CB), BF16),
        jax.ShapeDtypeStruct((N_LB, N, CB), BF16),
        jax.ShapeDtypeStruct((nTs, N_PB, POOL_BUF * R, CB), F32),
        jax.ShapeDtypeStruct((nTs, N_LB, CONV_BUF * R, CB), F32),
        jax.ShapeDtypeStruct((nTs, N_LB, R, CB), F32),
    )
    st_specs = (
        pl.BlockSpec((1, N_PB, POOL_BUF * R, CB), lambda i: (i, 0, 0, 0)),
        pl.BlockSpec((1, N_LB, CONV_BUF * R, CB), lambda i: (i, 0, 0, 0)),
        pl.BlockSpec((1, N_LB, R, CB), lambda i: (i, 0, 0, 0)),
    )
    out_specs = (
        pl.BlockSpec((N_PB, TM, CB), lambda i: (0, i, 0)),
        pl.BlockSpec((N_LB, TM, CB), lambda i: (0, i, 0)),
    ) + st_specs
    scratch = [
        pltpu.VMEM((TM, D), BF16),
        pltpu.VMEM(((POOL_BUF + S) * R, CB), F32),
        pltpu.VMEM(((CONV_BUF + S) * R, CB), F32),
        pltpu.VMEM((TM, CB), F32),
        pltpu.VMEM((TM, CB), F32),
    ]
    return pl.pallas_call(
        functools.partial(_mixer_sample_kernel, S, R, pos0),
        grid=(nTs,),
        in_specs=[pl.BlockSpec((1, TM, D), lambda i: (i, 0, 0))] + list(st_specs) + _mixer_weight_specs(),
        out_specs=out_specs,
        out_shape=out_shape,
        scratch_shapes=scratch,
        compiler_params=pltpu.CompilerParams(
            dimension_semantics=("arbitrary",), vmem_limit_bytes=VMEM_LIMIT_BYTES),
        name="mixer_sample",
    )(xs, pin, cin, hin, *mw)


def _merge(x2, ya, yb, gw, alpha):
    N, D = x2.shape
    TM = MERGE_ROWS
    return pl.pallas_call(
        functools.partial(_merge_kernel, alpha),
        grid=(N // TM,),
        in_specs=[
            pl.BlockSpec((TM, D), lambda i: (i, 0)),
            pl.BlockSpec((N_PB, TM, CB), lambda i: (0, i, 0)),
            pl.BlockSpec((N_LB, TM, CB), lambda i: (0, i, 0)),
        ] + [_vmem_spec() for _ in range(8)],
        out_specs=pl.BlockSpec((TM, D), lambda i: (i, 0)),
        out_shape=jax.ShapeDtypeStruct((N, D), F32),
        scratch_shapes=[pltpu.VMEM((TM, D), BF16)],
        compiler_params=pltpu.CompilerParams(
            dimension_semantics=("arbitrary",), vmem_limit_bytes=VMEM_LIMIT_BYTES),
        name="merge",
    )(x2, ya, yb, *gw)


def _col_blocks(w, lo, n):
    K = w.shape[0]
    return w[:, lo:lo + n * CB].reshape(K, n, CB).transpose(1, 0, 2).astype(BF16)


def _chan_blocks(v, n):
    v2 = v.reshape(-1, n * CB)
    return v2.reshape(v2.shape[0], n, CB).transpose(1, 0, 2)


def _block_diag_pairs(w):
    nb = w.shape[0]
    w2 = w.reshape(nb // 2, 2, LRU_BLOCK_DIM, LRU_BLOCK_DIM)
    z = jnp.zeros_like(w2[:, 0])
    top = jnp.concatenate([w2[:, 0], z], axis=2)
    bot = jnp.concatenate([z, w2[:, 1]], axis=2)
    return jnp.concatenate([top, bot], axis=1).astype(BF16)


def _layer_weights(l, w_in, b_merge, pool_w, pool_scale, conv_w, conv_b, lru_wa, lru_ba,
                   lru_wx, lru_bx, lru_L, w_proj_a, w_proj_b, w_out, ln_g, ln_b):
    wi = w_in[l]
    c_ga = POOL_WIDTH
    c_ub = 2 * POOL_WIDTH
    c_gb = c_ub + LRU_WIDTH
    c_ma = c_gb + LRU_WIDTH
    c_mb = c_ma + D_MODEL
    mixer = (
        _col_blocks(wi, 0, N_PB), _col_blocks(wi, c_ga, N_PB),
        _col_blocks(wi, c_ub, N_LB), _col_blocks(wi, c_gb, N_LB),
        pool_w[l].astype(BF16), _chan_blocks(pool_scale[l], N_PB),
        _chan_blocks(conv_w[l], N_LB), _chan_blocks(conv_b[l], N_LB),
        _block_diag_pairs(lru_wa[l]), _block_diag_pairs(lru_wx[l]),
        _chan_blocks(lru_ba[l], N_LB), _chan_blocks(lru_bx[l], N_LB), _chan_blocks(lru_L[l], N_LB),
    )
    merge = (
        wi[:, c_ma:c_mb].astype(BF16), wi[:, c_mb:].astype(BF16), b_merge[l],
        w_proj_a[l].reshape(N_PB, CB, D_MODEL).astype(BF16),
        w_proj_b[l].reshape(N_LB, CB, D_MODEL).astype(BF16),
        w_out[l].astype(BF16), ln_g[l].reshape(1, D_MODEL), ln_b[l].reshape(1, D_MODEL),
    )
    return mixer, merge


def kernel(x_prompt, x_sample, state_pool, state_conv, state_h, w_in, b_merge, pool_w, pool_scale, conv_w, conv_b, lru_wa, lru_ba, lru_wx, lru_bx, lru_L, w_proj_a, w_proj_b, w_out, ln_g, ln_b):
    depth = w_in.shape[0]
    alpha = (2.0 * depth) ** 0.25
    B, T, D = x_prompt.shape
    Bs, Ts, _ = x_sample.shape
    past_len = 16384

    S = PROMPT_STEPS
    TM = S * SEG
    nT = T // TM
    xp = x_prompt.reshape(B, nT, SEG, S, D).transpose(0, 1, 3, 2, 4).reshape(B, nT, TM, D)

    R = SAMPLE_ROWS
    nTs = Bs // R
    xs = x_sample.reshape(nTs, R, Ts, D).transpose(0, 2, 1, 3).reshape(nTs, Ts * R, D)

    def to_steps(st, n):
        k = st.shape[1]
        return st.reshape(nTs, R, k, n, CB).transpose(0, 3, 2, 1, 4).reshape(nTs, n, k * R, CB)

    def from_steps(st, k):
        n = st.shape[1]
        return st.reshape(nTs, n, k, R, CB).transpose(0, 3, 2, 1, 4).reshape(Bs, k, n * CB)

    pool_p, conv_p, h_p, pool_s, conv_s, h_s = [], [], [], [], [], []
    for l in range(depth):
        mw, gw = _layer_weights(l, w_in, b_merge, pool_w, pool_scale, conv_w, conv_b, lru_wa,
                                lru_ba, lru_wx, lru_bx, lru_L, w_proj_a, w_proj_b, w_out, ln_g, ln_b)
        ya, yb, pst, cst, hst = _mixer_prompt(xp, mw)
        xp = _merge(xp.reshape(B * T, D), ya, yb, gw, alpha).reshape(B, nT, TM, D)
        pool_p.append(pst.transpose(0, 2, 1, 3).reshape(B, POOL_BUF, POOL_WIDTH))
        conv_p.append(cst.transpose(0, 2, 1, 3).reshape(B, CONV_BUF, LRU_WIDTH))
        h_p.append(hst.transpose(0, 2, 1, 3).reshape(B, LRU_WIDTH))

        ya, yb, pst, cst, hst = _mixer_sample(
            xs, to_steps(state_pool[l], N_PB), to_steps(state_conv[l], N_LB),
            to_steps(state_h[l][:, None, :], N_LB), mw, Ts, past_len)
        xs = _merge(xs.reshape(Bs * Ts, D), ya, yb, gw, alpha).reshape(nTs, Ts * R, D)
        pool_s.append(from_steps(pst, POOL_BUF))
        conv_s.append(from_steps(cst, CONV_BUF))
        h_s.append(from_steps(hst, 1).reshape(Bs, LRU_WIDTH))

    y_prompt = xp.reshape(B, nT, S, SEG, D).transpose(0, 1, 3, 2, 4).reshape(B, T, D)
    y_sample = xs.reshape(nTs, Ts, R, D).transpose(0, 2, 1, 3).reshape(Bs, Ts, D)
    return (y_prompt, y_sample,
            jnp.stack(pool_p), jnp.stack(conv_p), jnp.stack(h_p),
            jnp.stack(pool_s), jnp.stack(conv_s), jnp.stack(h_s))
```

```python
import functools

import jax
import jax.numpy as jnp
from jax import lax
from jax.experimental import pallas as pl
from jax.experimental.pallas import tpu as pltpu

D_MODEL = 2048
POOL_WIDTH = D_MODEL // 2
POOL_WINDOWS = (2, 4, 8, 16)
POOL_BUF = max(POOL_WINDOWS) - 1
LRU_WIDTH = D_MODEL
LRU_BLOCK_DIM = 128
CONV_WIDTH = 4
CONV_BUF = CONV_WIDTH - 1
LRU_C = 8.0
LN_EPS = 1e-5
PAST_LEN = 16384

CB = 256
N_PB = POOL_WIDTH // CB
N_LB = LRU_WIDTH // CB
SEG = 8
PROMPT_STEPS = 64
SAMPLE_ROWS = 32
MERGE_ROWS = 256
MERGE_COLS = 512
VMEM_LIMIT_BYTES = 60 * 1024 * 1024

F32 = jnp.float32
BF16 = jnp.bfloat16


def _silu(v):
    return v * jax.nn.sigmoid(v)


def _dot(a, b):
    return jnp.dot(a, b, preferred_element_type=F32)


def _inv_count(rows, R, S, window, pos0, chained):
    assert R & (R - 1) == 0
    i = lax.broadcasted_iota(jnp.int32, (rows, 1), 0)
    step, seg = lax.shift_right_logical(i, R.bit_length() - 1), lax.bitwise_and(i, R - 1)
    pos = pos0 + (seg * S + step if chained else step)
    return 1.0 / jnp.minimum(pos + 1, window).astype(F32)


def _produce(idx, xb_s, wug, e_s, g_s, R, S):
    res = _dot(xb_s[...], wug[idx])
    e_s[pl.ds(POOL_BUF * R, S * R), :] = res[:, :CB]
    g_s[...] = res[:, CB:]


def _pool_mix(g, R, S, e_s, g_s, poolw, pscale, inv_cnt, ya_ref):
    TM = S * R
    H = POOL_BUF * R
    u = e_s[pl.ds(H, TM), :]
    acc = u
    for k in range(1, POOL_WINDOWS[g]):
        acc = acc + e_s[pl.ds(H - k * R, TM), :]
    diff = acc * inv_cnt - u
    mixed = _dot(diff.astype(BF16), poolw[g]) * pscale[g]
    ya_ref[g] = (mixed * _silu(g_s[...])).astype(BF16)


def _lru_gates(blk, R, S, e_s, convw, convb, wabd, wxbd, ba, bx, lL, a_s, b_s):
    TM = S * R
    H0 = (POOL_BUF - CONV_BUF) * R
    cw = convw[blk]
    xc = convb[blk] + e_s[pl.ds(H0, TM), :] * cw[0:1, :]
    for k in range(1, CONV_WIDTH):
        xc = xc + e_s[pl.ds(H0 + k * R, TM), :] * cw[k:k + 1, :]
    xcb = xc.astype(BF16)
    r = jax.nn.sigmoid(_dot(xcb, wabd[blk]) + ba[blk])
    i = jax.nn.sigmoid(_dot(xcb, wxbd[blk]) + bx[blk])
    nsp = -LRU_C * jax.nn.softplus(-lL[blk])
    log_a = r * nsp
    a = jnp.exp(log_a)
    a_s[...] = a
    b_s[...] = jnp.sqrt((1.0 - a) * (1.0 + a)) * (i * xc)


def _halo_from_tail(tail, carry, n):
    t3 = tail.reshape(n, SEG, CB)
    sub = lax.broadcasted_iota(jnp.int32, (n, SEG, CB), 1)
    return jnp.where(sub == 0, carry.reshape(n, SEG, CB), pltpu.roll(t3, 1, 1)).reshape(n * SEG, CB)


def _last_seg_rows(tail, n):
    return tail.reshape(n, SEG, CB)[:, SEG - 1, :]


def _bcast_last_seg(tail, n):
    t3 = tail.reshape(n, SEG, CB)
    return jnp.broadcast_to(t3[:, SEG - 1:SEG, :], (n, SEG, CB)).reshape(n * SEG, CB)


def _run_pipeline(produce, pool_item, lru_item, slots):
    n = N_PB + N_LB
    produce(0, slots[0])
    for k in range(n):
        if k + 1 < n:
            produce(k + 1, slots[(k + 1) % 2])
        if k < N_PB:
            pool_item(k, slots[k % 2])
        else:
            lru_item(k - N_PB, slots[k % 2])


def _mixer_prompt_kernel(S, x_ref, wug, poolw, pscale, convw, convb, wabd, wxbd, ba, bx, lL,
                         ya_ref, yb_ref, pst_ref, cst_ref, hst_ref,
                         xb_s, e0_s, e1_s, g0_s, g1_s, a0_s, a1_s, b0_s, b1_s, pc_s, cc_s, hc_s):
    R = SEG
    TM = S * R
    H = POOL_BUF * R
    t = pl.program_id(1)

    @pl.when(t == 0)
    def _():
        pc_s[...] = jnp.zeros_like(pc_s)
        cc_s[...] = jnp.zeros_like(cc_s)
        hc_s[...] = jnp.zeros_like(hc_s)

    xb_s[...] = x_ref[0, 0].astype(BF16)
    sub = lax.broadcasted_iota(jnp.int32, (R, CB), 0)

    def produce(idx, slot):
        _produce(idx, xb_s, wug, slot[0], slot[1], R, S)

    def pool_item(g, slot):
        e_s, g_s = slot[0], slot[1]
        n = POOL_BUF
        tail = e_s[pl.ds(H + TM - n * R, n * R), :]
        e_s[pl.ds(0, n * R), :] = _halo_from_tail(tail, pc_s[g], n)
        pc_s[g] = _bcast_last_seg(tail, n)
        pst_ref[0, g] = _last_seg_rows(tail, n)
        inv_cnt = _inv_count(TM, R, S, POOL_WINDOWS[g], t * TM, True)
        _pool_mix(g, R, S, e_s, g_s, poolw, pscale, inv_cnt, ya_ref)

    def lru_item(blk, slot):
        e_s, g_s, a_s, b_s = slot
        n = CONV_BUF
        tail = e_s[pl.ds(H + TM - n * R, n * R), :]
        e_s[pl.ds(H - n * R, n * R), :] = _halo_from_tail(tail, cc_s[blk], n)
        cc_s[blk] = _bcast_last_seg(tail, n)
        cst_ref[0, blk] = _last_seg_rows(tail, n)
        _lru_gates(blk, R, S, e_s, convw, convb, wabd, wxbd, ba, bx, lL, a_s, b_s)

        h_end = b_s[pl.ds(0, R), :]
        p_end = a_s[pl.ds(0, R), :]
        for j in range(1, S):
            rows = pl.ds(j * R, R)
            a = a_s[rows, :]
            h_end = a * h_end + b_s[rows, :]
            p_end = a * p_end
            b_s[rows, :] = h_end
            a_s[rows, :] = p_end

        cin = hc_s[blk]
        c = cin
        for s in range(1, R + 1):
            cin = (jnp.broadcast_to(p_end[s - 1:s, :], (R, CB)) * cin
                   + jnp.broadcast_to(h_end[s - 1:s, :], (R, CB)))
            if s < R:
                c = jnp.where(sub == s, cin, c)
        hc_s[blk] = cin
        hst_ref[0, blk] = cin[0:1, :]

        h = b_s[...].reshape(S, R, CB) + a_s[...].reshape(S, R, CB) * c[None]
        yb_ref[blk] = (h.reshape(TM, CB) * _silu(g_s[...])).astype(BF16)

    _run_pipeline(produce, pool_item, lru_item,
                  ((e0_s, g0_s, a0_s, b0_s), (e1_s, g1_s, a1_s, b1_s)))


def _mixer_sample_kernel(S, R, pos0, x_ref, pin_ref, cin_ref, hin_ref,
                         wug, poolw, pscale, convw, convb, wabd, wxbd, ba, bx, lL,
                         ya_ref, yb_ref, pst_ref, cst_ref, hst_ref,
                         xb_s, e0_s, e1_s, g0_s, g1_s, a0_s, a1_s, b0_s, b1_s):
    TM = S * R
    H = POOL_BUF * R
    xb_s[...] = x_ref[0].astype(BF16)

    def produce(idx, slot):
        _produce(idx, xb_s, wug, slot[0], slot[1], R, S)

    def pool_item(g, slot):
        e_s, g_s = slot[0], slot[1]
        e_s[pl.ds(0, H), :] = pin_ref[0, g]
        pst_ref[0, g] = e_s[pl.ds(TM, H), :]
        inv_cnt = _inv_count(TM, R, S, POOL_WINDOWS[g], pos0, False)
        _pool_mix(g, R, S, e_s, g_s, poolw, pscale, inv_cnt, ya_ref)

    def lru_item(blk, slot):
        e_s, g_s, a_s, b_s = slot
        n = CONV_BUF
        e_s[pl.ds(H - n * R, n * R), :] = cin_ref[0, blk]
        cst_ref[0, blk] = e_s[pl.ds(H + TM - n * R, n * R), :]
        _lru_gates(blk, R, S, e_s, convw, convb, wabd, wxbd, ba, bx, lL, a_s, b_s)

        h = hin_ref[0, blk]
        for j in range(S):
            rows = pl.ds(j * R, R)
            h = a_s[rows, :] * h + b_s[rows, :]
            b_s[rows, :] = h
        hst_ref[0, blk] = h
        yb_ref[blk] = (b_s[...] * _silu(g_s[...])).astype(BF16)

    _run_pipeline(produce, pool_item, lru_item,
                  ((e0_s, g0_s, a0_s, b0_s), (e1_s, g1_s, a1_s, b1_s)))


def _merge_kernel(alpha, x_ref, ya_ref, yb_ref, wma, wmb, bm, wpa, wpb, wout, lng, lnb,
                  o_ref, m_s):
    xf = x_ref[...]
    xb = xf.astype(BF16)
    for j in range(D_MODEL // MERGE_COLS):
        cs = slice(j * MERGE_COLS, (j + 1) * MERGE_COLS)
        pa = _dot(ya_ref[0], wpa[0, :, cs])
        for g in range(1, N_PB):
            pa = pa + _dot(ya_ref[g], wpa[g, :, cs])
        pb = _dot(yb_ref[0], wpb[0, :, cs])
        for g in range(1, N_LB):
            pb = pb + _dot(yb_ref[g], wpb[g, :, cs])
        g_a = jax.nn.sigmoid(_dot(xb, wma[:, cs]) + bm[0:1, cs])
        g_b = jax.nn.sigmoid(_dot(xb, wmb[:, cs]) + bm[1:2, cs])
        m_s[:, cs] = (g_a * pa + g_b * pb).astype(BF16)
    out = _dot(m_s[...], wout[...])
    z = alpha * xf + out
    mu = jnp.mean(z, axis=-1, keepdims=True)
    zc = z - mu
    var = jnp.mean(zc * zc, axis=-1, keepdims=True)
    o_ref[...] = zc * lax.rsqrt(var + LN_EPS) * lng[...] + lnb[...]


def _vmem_spec():
    return pl.BlockSpec(memory_space=pltpu.VMEM)


N_MIXER_WEIGHTS = 10


def _slot_scratch(R, S):
    TM = S * R
    e = pltpu.VMEM(((POOL_BUF + S) * R, CB), F32)
    t = pltpu.VMEM((TM, CB), F32)
    return [e, e, t, t, t, t, t, t]


def _mixer_prompt(xp, mw):
    B, nT, TM, D = xp.shape
    S = TM // SEG
    N = B * nT * TM
    out_shape = (
        jax.ShapeDtypeStruct((N_PB, N, CB), BF16),
        jax.ShapeDtypeStruct((N_LB, N, CB), BF16),
        jax.ShapeDtypeStruct((B, N_PB, POOL_BUF, CB), F32),
        jax.ShapeDtypeStruct((B, N_LB, CONV_BUF, CB), F32),
        jax.ShapeDtypeStruct((B, N_LB, 1, CB), F32),
    )
    out_specs = (
        pl.BlockSpec((N_PB, TM, CB), lambda b, t: (0, b * nT + t, 0)),
        pl.BlockSpec((N_LB, TM, CB), lambda b, t: (0, b * nT + t, 0)),
        pl.BlockSpec((1, N_PB, POOL_BUF, CB), lambda b, t: (b, 0, 0, 0)),
        pl.BlockSpec((1, N_LB, CONV_BUF, CB), lambda b, t: (b, 0, 0, 0)),
        pl.BlockSpec((1, N_LB, 1, CB), lambda b, t: (b, 0, 0, 0)),
    )
    scratch = [pltpu.VMEM((TM, D), BF16)] + _slot_scratch(SEG, S) + [
        pltpu.VMEM((N_PB, POOL_BUF * SEG, CB), F32),
        pltpu.VMEM((N_LB, CONV_BUF * SEG, CB), F32),
        pltpu.VMEM((N_LB, SEG, CB), F32),
    ]
    return pl.pallas_call(
        functools.partial(_mixer_prompt_kernel, S),
        grid=(B, nT),
        in_specs=[pl.BlockSpec((1, 1, TM, D), lambda b, t: (b, t, 0, 0))]
        + [_vmem_spec() for _ in range(N_MIXER_WEIGHTS)],
        out_specs=out_specs,
        out_shape=out_shape,
        scratch_shapes=scratch,
        compiler_params=pltpu.CompilerParams(
            dimension_semantics=("arbitrary", "arbitrary"), vmem_limit_bytes=VMEM_LIMIT_BYTES),
        name="mixer_prompt",
    )(xp, *mw)


def _mixer_sample(xs, pin, cin, hin, mw, S, pos0):
    nTs, TM, D = xs.shape
    R = TM // S
    N = nTs * TM
    out_shape = (
        jax.ShapeDtypeStruct((N_PB, N, CB), BF16),
        jax.ShapeDtypeStruct((N_LB, N, CB), BF16),
        jax.ShapeDtypeStruct((nTs, N_PB, POOL_BUF * R, CB), F32),
        jax.ShapeDtypeStruct((nTs, N_LB, CONV_BUF * R, CB), F32),
        jax.ShapeDtypeStruct((nTs, N_LB, R, CB), F32),
    )
    st_specs = (
        pl.BlockSpec((1, N_PB, POOL_BUF * R, CB), lambda i: (i, 0, 0, 0)),
        pl.BlockSpec((1, N_LB, CONV_BUF * R, CB), lambda i: (i, 0, 0, 0)),
        pl.BlockSpec((1, N_LB, R, CB), lambda i: (i, 0, 0, 0)),
    )
    out_specs = (
        pl.BlockSpec((N_PB, TM, CB), lambda i: (0, i, 0)),
        pl.BlockSpec((N_LB, TM, CB), lambda i: (0, i, 0)),
    ) + st_specs
    scratch = [pltpu.VMEM((TM, D), BF16)] + _slot_scratch(R, S)
    return pl.pallas_call(
        functools.partial(_mixer_sample_kernel, S, R, pos0),
        grid=(nTs,),
        in_specs=[pl.BlockSpec((1, TM, D), lambda i: (i, 0, 0))] + list(st_specs)
        + [_vmem_spec() for _ in range(N_MIXER_WEIGHTS)],
        out_specs=out_specs,
        out_shape=out_shape,
        scratch_shapes=scratch,
        compiler_params=pltpu.CompilerParams(
            dimension_semantics=("arbitrary",), vmem_limit_bytes=VMEM_LIMIT_BYTES),
        name="mixer_sample",
    )(xs, pin, cin, hin, *mw)


def _merge(x2, ya, yb, gw, alpha):
    N, D = x2.shape
    TM = MERGE_ROWS
    return pl.pallas_call(
        functools.partial(_merge_kernel, alpha),
        grid=(N // TM,),
        in_specs=[
            pl.BlockSpec((TM, D), lambda i: (i, 0)),
            pl.BlockSpec((N_PB, TM, CB), lambda i: (0, i, 0)),
            pl.BlockSpec((N_LB, TM, CB), lambda i: (0, i, 0)),
        ] + [_vmem_spec() for _ in range(8)],
        out_specs=pl.BlockSpec((TM, D), lambda i: (i, 0)),
        out_shape=jax.ShapeDtypeStruct((N, D), F32),
        scratch_shapes=[pltpu.VMEM((TM, D), BF16)],
        compiler_params=pltpu.CompilerParams(
            dimension_semantics=("arbitrary",), vmem_limit_bytes=VMEM_LIMIT_BYTES),
        name="merge",
    )(x2, ya, yb, *gw)


def _ug_blocks(w, lo_u, lo_g, n):
    K = w.shape[0]
    u = w[:, lo_u:lo_u + n * CB].reshape(K, n, CB)
    g = w[:, lo_g:lo_g + n * CB].reshape(K, n, CB)
    return jnp.concatenate([u, g], axis=2).transpose(1, 0, 2).astype(BF16)


def _chan_blocks(v, n):
    v2 = v.reshape(-1, n * CB)
    return v2.reshape(v2.shape[0], n, CB).transpose(1, 0, 2)


def _block_diag_pairs(w):
    nb = w.shape[0]
    w2 = w.reshape(nb // 2, 2, LRU_BLOCK_DIM, LRU_BLOCK_DIM)
    z = jnp.zeros_like(w2[:, 0])
    top = jnp.concatenate([w2[:, 0], z], axis=2)
    bot = jnp.concatenate([z, w2[:, 1]], axis=2)
    return jnp.concatenate([top, bot], axis=1).astype(BF16)


def _layer_weights(l, w_in, b_merge, pool_w, pool_scale, conv_w, conv_b, lru_wa, lru_ba,
                   lru_wx, lru_bx, lru_L, w_proj_a, w_proj_b, w_out, ln_g, ln_b):
    wi = w_in[l]
    c_ga = POOL_WIDTH
    c_ub = 2 * POOL_WIDTH
    c_gb = c_ub + LRU_WIDTH
    c_ma = c_gb + LRU_WIDTH
    c_mb = c_ma + D_MODEL
    mixer = (
        jnp.concatenate([_ug_blocks(wi, 0, c_ga, N_PB), _ug_blocks(wi, c_ub, c_gb, N_LB)], axis=0),
        pool_w[l].astype(BF16), _chan_blocks(pool_scale[l], N_PB),
        _chan_blocks(conv_w[l], N_LB), _chan_blocks(conv_b[l], N_LB),
        _block_diag_pairs(lru_wa[l]), _block_diag_pairs(lru_wx[l]),
        _chan_blocks(lru_ba[l], N_LB), _chan_blocks(lru_bx[l], N_LB), _chan_blocks(lru_L[l], N_LB),
    )
    assert len(mixer) == N_MIXER_WEIGHTS
    merge = (
        wi[:, c_ma:c_mb].astype(BF16), wi[:, c_mb:].astype(BF16), b_merge[l],
        w_proj_a[l].reshape(N_PB, CB, D_MODEL).astype(BF16),
        w_proj_b[l].reshape(N_LB, CB, D_MODEL).astype(BF16),
        w_out[l].astype(BF16), ln_g[l].reshape(1, D_MODEL), ln_b[l].reshape(1, D_MODEL),
    )
    return mixer, merge


def kernel(x_prompt, x_sample, state_pool, state_conv, state_h, w_in, b_merge, pool_w, pool_scale, conv_w, conv_b, lru_wa, lru_ba, lru_wx, lru_bx, lru_L, w_proj_a, w_proj_b, w_out, ln_g, ln_b):
    depth = w_in.shape[0]
    alpha = (2.0 * depth) ** 0.25
    B, T, D = x_prompt.shape
    Bs, Ts, _ = x_sample.shape

    S = PROMPT_STEPS
    TM = S * SEG
    nT = T // TM
    xp = x_prompt.reshape(B, nT, SEG, S, D).transpose(0, 1, 3, 2, 4).reshape(B, nT, TM, D)

    R = SAMPLE_ROWS
    nTs = Bs // R
    xs = x_sample.reshape(nTs, R, Ts, D).transpose(0, 2, 1, 3).reshape(nTs, Ts * R, D)

    def to_steps(st, n):
        k = st.shape[1]
        return st.reshape(nTs, R, k, n, CB).transpose(0, 3, 2, 1, 4).reshape(nTs, n, k * R, CB)

    def from_steps(st, k):
        n = st.shape[1]
        return st.reshape(nTs, n, k, R, CB).transpose(0, 3, 2, 1, 4).reshape(Bs, k, n * CB)

    pool_p, conv_p, h_p, pool_s, conv_s, h_s = [], [], [], [], [], []
    for l in range(depth):
        mw, gw = _layer_weights(l, w_in, b_merge, pool_w, pool_scale, conv_w, conv_b, lru_wa,
                                lru_ba, lru_wx, lru_bx, lru_L, w_proj_a, w_proj_b, w_out, ln_g, ln_b)
        ya, yb, pst, cst, hst = _mixer_prompt(xp, mw)
        xp = _merge(xp.reshape(B * T, D), ya, yb, gw, alpha).reshape(B, nT, TM, D)
        pool_p.append(pst.transpose(0, 2, 1, 3).reshape(B, POOL_BUF, POOL_WIDTH))
        conv_p.append(cst.transpose(0, 2, 1, 3).reshape(B, CONV_BUF, LRU_WIDTH))
        h_p.append(hst.transpose(0, 2, 1, 3).reshape(B, LRU_WIDTH))

        ya, yb, pst, cst, hst = _mixer_sample(
            xs, to_steps(state_pool[l], N_PB), to_steps(state_conv[l], N_LB),
            to_steps(state_h[l][:, None, :], N_LB), mw, Ts, PAST_LEN)
        xs = _merge(xs.reshape(Bs * Ts, D), ya, yb, gw, alpha).reshape(nTs, Ts * R, D)
        pool_s.append(from_steps(pst, POOL_BUF))
        conv_s.append(from_steps(cst, CONV_BUF))
        h_s.append(from_steps(hst, 1).reshape(Bs, LRU_WIDTH))

    y_prompt = xp.reshape(B, nT, S, SEG, D).transpose(0, 1, 3, 2, 4).reshape(B, T, D)
    y_sample = xs.reshape(nTs, Ts, R, D).transpose(0, 2, 1, 3).reshape(Bs, Ts, D)
    return (y_prompt, y_sample,
            jnp.stack(pool_p), jnp.stack(conv_p), jnp.stack(h_p),
            jnp.stack(pool_s), jnp.stack(conv_s), jnp.stack(h_s))
```

```python
import functools

import jax
import jax.numpy as jnp
from jax import lax
from jax.experimental import pallas as pl
from jax.experimental.pallas import tpu as pltpu

D_MODEL = 2048
POOL_WIDTH = D_MODEL // 2
POOL_WINDOWS = (2, 4, 8, 16)
POOL_BUF = max(POOL_WINDOWS) - 1
LRU_WIDTH = D_MODEL
LRU_BLOCK_DIM = 128
CONV_WIDTH = 4
CONV_BUF = CONV_WIDTH - 1
LRU_C = 8.0
LN_EPS = 1e-5
PAST_LEN = 16384

COL_GA = POOL_WIDTH
COL_UB = 2 * POOL_WIDTH
COL_GB = COL_UB + LRU_WIDTH
COL_M = COL_GB + LRU_WIDTH

CB = 256
N_PB = POOL_WIDTH // CB
N_LB = LRU_WIDTH // CB
SEG = 8
PROMPT_STEPS = 64
SAMPLE_ROWS = 32
MERGE_STEPS = 32
MERGE_COLS = 512
VMEM_LIMIT_BYTES = 60 * 1024 * 1024

F32 = jnp.float32
BF16 = jnp.bfloat16


def _silu(v):
    return v * jax.nn.sigmoid(v)


def _dot(a, b):
    return jnp.dot(a, b, preferred_element_type=F32)


def _copies(views, sems, slot):
    return [pltpu.make_async_copy(src, dst, sems.at[slot, n]) for n, (src, dst) in enumerate(views)]


def _stream_in(i, n, views_fn, sems):
    slot = lax.rem(i, 2)

    @pl.when(i == 0)
    def _():
        for c in _copies(views_fn(i, slot), sems, slot):
            c.start()

    @pl.when(i + 1 < n)
    def _():
        for c in _copies(views_fn(i + 1, 1 - slot), sems, 1 - slot):
            c.start()

    for c in _copies(views_fn(i, slot), sems, slot):
        c.wait()
    return slot


def _stream_out(i, n, views_fn, sems):
    slot = lax.rem(i, 2)
    for c in _copies(views_fn(i, slot), sems, slot):
        c.start()

    @pl.when(i >= 1)
    def _():
        for c in _copies(views_fn(i - 1, 1 - slot), sems, 1 - slot):
            c.wait()

    @pl.when(i == n - 1)
    def _():
        for c in _copies(views_fn(i, slot), sems, slot):
            c.wait()


def _prompt_views(hbm, buf, nT, TM, S, steps):
    per = S // steps

    def views(i, slot, to_hbm=False):
        m = i // per
        b = m // nT
        row0 = (m % nT) * TM + (i % per) * steps
        pairs = [(hbm.at[b, pl.ds(row0 + s * S, steps), :], buf.at[slot, :, s, :]) for s in range(SEG)]
        return [(d, h) for h, d in pairs] if to_hbm else pairs

    return views


def _sample_views(hbm, buf, R, n_steps):
    C = buf.shape[-1]

    def views(i, slot, to_hbm=False):
        pairs = [(hbm.at[pl.ds(i * R, R), pl.ds(k * C, C)], buf.at[slot, pl.ds(k * R, R), :])
                 for k in range(n_steps)]
        return [(d, h) for h, d in pairs] if to_hbm else pairs

    return views


def _inv_count(rows, R, S, window, pos0, chained):
    assert R & (R - 1) == 0
    i = lax.broadcasted_iota(jnp.int32, (rows, 1), 0)
    step, seg = lax.shift_right_logical(i, R.bit_length() - 1), lax.bitwise_and(i, R - 1)
    pos = pos0 + (seg * S + step if chained else step)
    return 1.0 / jnp.minimum(pos + 1, window).astype(F32)


def _block_cols(k):
    if k < N_PB:
        return k * CB, COL_GA + k * CB, k * CB
    c = (k - N_PB) * CB
    return COL_UB + c, COL_GB + c, c


def _produce(k, xb_s, wmix, e_s, g_s, R, S):
    cu, cg, _ = _block_cols(k)
    xb = xb_s[...]
    e_s[pl.ds(POOL_BUF * R, S * R), :] = _dot(xb, wmix[:, cu:cu + CB])
    g_s[...] = _dot(xb, wmix[:, cg:cg + CB])


def _pool_mix(g, R, S, e_s, g_s, poolw, pscale, inv_cnt, ya_ref):
    TM = S * R
    H = POOL_BUF * R
    u = e_s[pl.ds(H, TM), :]
    acc = u
    for k in range(1, POOL_WINDOWS[g]):
        acc = acc + e_s[pl.ds(H - k * R, TM), :]
    diff = acc * inv_cnt - u
    mixed = _dot(diff.astype(BF16), poolw[g]) * pscale[:, g * CB:(g + 1) * CB]
    ya_ref[g] = (mixed * _silu(g_s[...])).astype(BF16)


def _lru_gates(blk, R, S, e_s, convw, convb, wabd, wxbd, ba, bx, lL, a_s, b_s):
    TM = S * R
    H0 = (POOL_BUF - CONV_BUF) * R
    cs = slice(blk * CB, (blk + 1) * CB)
    xc = convb[:, cs] + e_s[pl.ds(H0, TM), :] * convw[0:1, cs]
    for k in range(1, CONV_WIDTH):
        xc = xc + e_s[pl.ds(H0 + k * R, TM), :] * convw[k:k + 1, cs]
    xcb = xc.astype(BF16)
    r = jax.nn.sigmoid(_dot(xcb, wabd[blk]) + ba[:, cs])
    i = jax.nn.sigmoid(_dot(xcb, wxbd[blk]) + bx[:, cs])
    nsp = -LRU_C * jax.nn.softplus(-lL[:, cs])
    log_a = r * nsp
    a = jnp.exp(log_a)
    a_s[...] = a
    b_s[...] = jnp.sqrt((1.0 - a) * (1.0 + a)) * (i * xc)


def _halo_from_tail(tail, carry, n):
    t3 = tail.reshape(n, SEG, CB)
    sub = lax.broadcasted_iota(jnp.int32, (n, SEG, CB), 1)
    return jnp.where(sub == 0, carry.reshape(n, SEG, CB), pltpu.roll(t3, 1, 1)).reshape(n * SEG, CB)


def _last_seg_rows(tail, n):
    return tail.reshape(n, SEG, CB)[:, SEG - 1, :]


def _bcast_last_seg(tail, n):
    t3 = tail.reshape(n, SEG, CB)
    return jnp.broadcast_to(t3[:, SEG - 1:SEG, :], (n, SEG, CB)).reshape(n * SEG, CB)


def _run_pipeline(produce, pool_item, lru_item, slots):
    n = N_PB + N_LB
    produce(0, slots[0])
    for k in range(n):
        if k + 1 < n:
            produce(k + 1, slots[(k + 1) % 2])
        if k < N_PB:
            pool_item(k, slots[k % 2])
        else:
            lru_item(k - N_PB, slots[k % 2])


def _mixer_prompt_kernel(S, nT, n_tiles, x_hbm, wmix, poolw, pscale, convw, convb, wabd, wxbd,
                         ba, bx, lL,
                         ya_ref, yb_ref, pst_ref, cst_ref, hst_ref,
                         xbuf, xsem, xb_s, e0_s, e1_s, g0_s, g1_s, a0_s, a1_s, b0_s, b1_s,
                         pc_s, cc_s, hc_s):
    R = SEG
    TM = S * R
    H = POOL_BUF * R
    i = pl.program_id(0)
    t = lax.rem(i, nT)

    slot = _stream_in(i, n_tiles, _prompt_views(x_hbm, xbuf, nT, TM, S, S), xsem)

    @pl.when(t == 0)
    def _():
        pc_s[...] = jnp.zeros_like(pc_s)
        cc_s[...] = jnp.zeros_like(cc_s)
        hc_s[...] = jnp.zeros_like(hc_s)

    xb_s[...] = xbuf[slot].reshape(TM, D_MODEL).astype(BF16)
    sub = lax.broadcasted_iota(jnp.int32, (R, CB), 0)

    def produce(k, sl):
        _produce(k, xb_s, wmix, sl[0], sl[1], R, S)

    def pool_item(g, sl):
        e_s, g_s = sl[0], sl[1]
        n = POOL_BUF
        tail = e_s[pl.ds(H + TM - n * R, n * R), :]
        e_s[pl.ds(0, n * R), :] = _halo_from_tail(tail, pc_s[g], n)
        pc_s[g] = _bcast_last_seg(tail, n)
        pst_ref[0, g] = _last_seg_rows(tail, n)
        inv_cnt = _inv_count(TM, R, S, POOL_WINDOWS[g], t * TM, True)
        _pool_mix(g, R, S, e_s, g_s, poolw, pscale, inv_cnt, ya_ref)

    def lru_item(blk, sl):
        e_s, g_s, a_s, b_s = sl
        n = CONV_BUF
        tail = e_s[pl.ds(H + TM - n * R, n * R), :]
        e_s[pl.ds(H - n * R, n * R), :] = _halo_from_tail(tail, cc_s[blk], n)
        cc_s[blk] = _bcast_last_seg(tail, n)
        cst_ref[0, blk] = _last_seg_rows(tail, n)
        _lru_gates(blk, R, S, e_s, convw, convb, wabd, wxbd, ba, bx, lL, a_s, b_s)

        h_end = b_s[pl.ds(0, R), :]
        p_end = a_s[pl.ds(0, R), :]
        for j in range(1, S):
            rows = pl.ds(j * R, R)
            a = a_s[rows, :]
            h_end = a * h_end + b_s[rows, :]
            p_end = a * p_end
            b_s[rows, :] = h_end
            a_s[rows, :] = p_end

        cin = hc_s[blk]
        c = cin
        for s in range(1, R + 1):
            cin = (jnp.broadcast_to(p_end[s - 1:s, :], (R, CB)) * cin
                   + jnp.broadcast_to(h_end[s - 1:s, :], (R, CB)))
            if s < R:
                c = jnp.where(sub == s, cin, c)
        hc_s[blk] = cin
        hst_ref[0, blk] = cin[0:1, :]

        h = b_s[...].reshape(S, R, CB) + a_s[...].reshape(S, R, CB) * c[None]
        yb_ref[blk] = (h.reshape(TM, CB) * _silu(g_s[...])).astype(BF16)

    _run_pipeline(produce, pool_item, lru_item,
                  ((e0_s, g0_s, a0_s, b0_s), (e1_s, g1_s, a1_s, b1_s)))


def _mixer_sample_kernel(S, R, pos0, n_tiles, x_hbm, pin_hbm, cin_hbm, hin_ref,
                         wmix, poolw, pscale, convw, convb, wabd, wxbd, ba, bx, lL,
                         ya_ref, yb_ref, pout_hbm, cout_hbm, hst_ref,
                         xbuf, xsem, pbuf, psem, cbuf, csem, pobuf, posem, cobuf, cosem,
                         xb_s, e0_s, e1_s, g0_s, g1_s, a0_s, a1_s, b0_s, b1_s):
    TM = S * R
    H = POOL_BUF * R
    i = pl.program_id(0)
    keep_p = POOL_BUF - S
    assert keep_p > 0 and S >= CONV_BUF

    slot = _stream_in(i, n_tiles, _sample_views(x_hbm, xbuf, R, S), xsem)
    _stream_in(i, n_tiles, _sample_views(pin_hbm, pbuf, R, POOL_BUF), psem)
    _stream_in(i, n_tiles, _sample_views(cin_hbm, cbuf, R, CONV_BUF), csem)

    xb_s[...] = xbuf[slot].astype(BF16)
    pobuf[slot, pl.ds(0, keep_p * R), :] = pbuf[slot, pl.ds(S * R, keep_p * R), :]

    def produce(k, sl):
        _produce(k, xb_s, wmix, sl[0], sl[1], R, S)

    def pool_item(g, sl):
        e_s, g_s = sl[0], sl[1]
        cs = slice(g * CB, (g + 1) * CB)
        e_s[pl.ds(0, H), :] = pbuf[slot, :, cs]
        pobuf[slot, pl.ds(keep_p * R, TM), cs] = e_s[pl.ds(H, TM), :]
        inv_cnt = _inv_count(TM, R, S, POOL_WINDOWS[g], pos0, False)
        _pool_mix(g, R, S, e_s, g_s, poolw, pscale, inv_cnt, ya_ref)

    def lru_item(blk, sl):
        e_s, g_s, a_s, b_s = sl
        n = CONV_BUF
        cs = slice(blk * CB, (blk + 1) * CB)
        e_s[pl.ds(H - n * R, n * R), :] = cbuf[slot, :, cs]
        cobuf[slot, :, cs] = e_s[pl.ds(H + TM - n * R, n * R), :]
        _lru_gates(blk, R, S, e_s, convw, convb, wabd, wxbd, ba, bx, lL, a_s, b_s)

        h = hin_ref[:, cs]
        for j in range(S):
            rows = pl.ds(j * R, R)
            h = a_s[rows, :] * h + b_s[rows, :]
            b_s[rows, :] = h
        hst_ref[:, cs] = h
        yb_ref[blk] = (b_s[...] * _silu(g_s[...])).astype(BF16)

    _run_pipeline(produce, pool_item, lru_item,
                  ((e0_s, g0_s, a0_s, b0_s), (e1_s, g1_s, a1_s, b1_s)))

    pviews = _sample_views(pout_hbm, pobuf, R, POOL_BUF)
    cviews = _sample_views(cout_hbm, cobuf, R, CONV_BUF)
    _stream_out(i, n_tiles, functools.partial(pviews, to_hbm=True), posem)
    _stream_out(i, n_tiles, functools.partial(cviews, to_hbm=True), cosem)


def _slot_scratch(R, S):
    TM = S * R
    e = pltpu.VMEM(((POOL_BUF + S) * R, CB), F32)
    t = pltpu.VMEM((TM, CB), F32)
    return [e, e, t, t, t, t, t, t]


def _vmem_spec():
    return pl.BlockSpec(memory_space=pltpu.VMEM)


def _any_spec():
    return pl.BlockSpec(memory_space=pl.ANY)


N_MIXER_WEIGHTS = 10


def _mixer_prompt(x, mw):
    B, T, D = x.shape
    S = PROMPT_STEPS
    TM = S * SEG
    nT = T // TM
    n_tiles = B * nT
    N = B * T
    out_shape = (
        jax.ShapeDtypeStruct((N_PB, N, CB), BF16),
        jax.ShapeDtypeStruct((N_LB, N, CB), BF16),
        jax.ShapeDtypeStruct((B, N_PB, POOL_BUF, CB), F32),
        jax.ShapeDtypeStruct((B, N_LB, CONV_BUF, CB), F32),
        jax.ShapeDtypeStruct((B, N_LB, 1, CB), F32),
    )
    out_specs = (
        pl.BlockSpec((N_PB, TM, CB), lambda i: (0, i, 0)),
        pl.BlockSpec((N_LB, TM, CB), lambda i: (0, i, 0)),
        pl.BlockSpec((1, N_PB, POOL_BUF, CB), lambda i: (i // nT, 0, 0, 0)),
        pl.BlockSpec((1, N_LB, CONV_BUF, CB), lambda i: (i // nT, 0, 0, 0)),
        pl.BlockSpec((1, N_LB, 1, CB), lambda i: (i // nT, 0, 0, 0)),
    )
    scratch = [
        pltpu.VMEM((2, S, SEG, D), F32), pltpu.SemaphoreType.DMA((2, SEG)),
        pltpu.VMEM((TM, D), BF16),
    ] + _slot_scratch(SEG, S) + [
        pltpu.VMEM((N_PB, POOL_BUF * SEG, CB), F32),
        pltpu.VMEM((N_LB, CONV_BUF * SEG, CB), F32),
        pltpu.VMEM((N_LB, SEG, CB), F32),
    ]
    return pl.pallas_call(
        functools.partial(_mixer_prompt_kernel, S, nT, n_tiles),
        grid=(n_tiles,),
        in_specs=[_any_spec()] + [_vmem_spec() for _ in range(N_MIXER_WEIGHTS)],
        out_specs=out_specs,
        out_shape=out_shape,
        scratch_shapes=scratch,
        compiler_params=pltpu.CompilerParams(
            dimension_semantics=("arbitrary",), vmem_limit_bytes=VMEM_LIMIT_BYTES),
        name="mixer_prompt",
    )(x, *mw)


def _mixer_sample(x, spool, sconv, sh, mw, pos0):
    Bs = x.shape[0]
    D = D_MODEL
    S = x.shape[1] // D
    R = SAMPLE_ROWS
    TM = S * R
    n_tiles = Bs // R
    N = Bs * S
    out_shape = (
        jax.ShapeDtypeStruct((N_PB, N, CB), BF16),
        jax.ShapeDtypeStruct((N_LB, N, CB), BF16),
        jax.ShapeDtypeStruct(spool.shape, F32),
        jax.ShapeDtypeStruct(sconv.shape, F32),
        jax.ShapeDtypeStruct(sh.shape, F32),
    )
    h_spec = pl.BlockSpec((R, LRU_WIDTH), lambda i: (i, 0))
    out_specs = (
        pl.BlockSpec((N_PB, TM, CB), lambda i: (0, i, 0)),
        pl.BlockSpec((N_LB, TM, CB), lambda i: (0, i, 0)),
        _any_spec(), _any_spec(), h_spec,
    )
    scratch = [
        pltpu.VMEM((2, TM, D), F32), pltpu.SemaphoreType.DMA((2, S)),
        pltpu.VMEM((2, POOL_BUF * R, POOL_WIDTH), F32), pltpu.SemaphoreType.DMA((2, POOL_BUF)),
        pltpu.VMEM((2, CONV_BUF * R, LRU_WIDTH), F32), pltpu.SemaphoreType.DMA((2, CONV_BUF)),
        pltpu.VMEM((2, POOL_BUF * R, POOL_WIDTH), F32), pltpu.SemaphoreType.DMA((2, POOL_BUF)),
        pltpu.VMEM((2, CONV_BUF * R, LRU_WIDTH), F32), pltpu.SemaphoreType.DMA((2, CONV_BUF)),
        pltpu.VMEM((TM, D), BF16),
    ] + _slot_scratch(R, S)
    return pl.pallas_call(
        functools.partial(_mixer_sample_kernel, S, R, pos0, n_tiles),
        grid=(n_tiles,),
        in_specs=[_any_spec(), _any_spec(), _any_spec(), h_spec]
        + [_vmem_spec() for _ in range(N_MIXER_WEIGHTS)],
        out_specs=out_specs,
        out_shape=out_shape,
        scratch_shapes=scratch,
        compiler_params=pltpu.CompilerParams(
            dimension_semantics=("arbitrary",), vmem_limit_bytes=VMEM_LIMIT_BYTES),
        name="mixer_sample",
    )(x, spool, sconv, sh, *mw)


def _merge_kernel(alpha, n_tiles, make_views, x_hbm, ya_ref, yb_ref, wm, bm, wpa, wpb, wout, lng, lnb,
                  o_hbm, xbuf, xsem, obuf, osem, m_s):
    i = pl.program_id(0)
    TM = m_s.shape[0]
    slot = _stream_in(i, n_tiles, make_views(x_hbm, xbuf), xsem)
    xf = xbuf[slot].reshape(TM, D_MODEL)
    xb = xf.astype(BF16)
    for j in range(D_MODEL // MERGE_COLS):
        cs = slice(j * MERGE_COLS, (j + 1) * MERGE_COLS)
        cb = slice(D_MODEL + j * MERGE_COLS, D_MODEL + (j + 1) * MERGE_COLS)
        pa = _dot(ya_ref[0], wpa[0:CB, cs])
        for g in range(1, N_PB):
            pa = pa + _dot(ya_ref[g], wpa[g * CB:(g + 1) * CB, cs])
        pb = _dot(yb_ref[0], wpb[0:CB, cs])
        for g in range(1, N_LB):
            pb = pb + _dot(yb_ref[g], wpb[g * CB:(g + 1) * CB, cs])
        g_a = jax.nn.sigmoid(_dot(xb, wm[:, cs]) + bm[0:1, cs])
        g_b = jax.nn.sigmoid(_dot(xb, wm[:, cb]) + bm[1:2, cs])
        m_s[:, cs] = (g_a * pa + g_b * pb).astype(BF16)
    out = _dot(m_s[...], wout[...])
    z = alpha * xf + out
    mu = jnp.mean(z, axis=-1, keepdims=True)
    zc = z - mu
    var = jnp.mean(zc * zc, axis=-1, keepdims=True)
    y = zc * lax.rsqrt(var + LN_EPS) * lng[...] + lnb[...]
    obuf[slot] = y.reshape(obuf.shape[1:])
    _stream_out(i, n_tiles, functools.partial(make_views(o_hbm, obuf), to_hbm=True), osem)


def _merge(x, ya, yb, gw, alpha, make_views, buf_shape, n_copies):
    N = ya.shape[1]
    TM = buf_shape[0] * (buf_shape[1] if len(buf_shape) == 3 else 1)
    n_tiles = N // TM
    return pl.pallas_call(
        functools.partial(_merge_kernel, alpha, n_tiles, make_views),
        grid=(n_tiles,),
        in_specs=[
            _any_spec(),
            pl.BlockSpec((N_PB, TM, CB), lambda i: (0, i, 0)),
            pl.BlockSpec((N_LB, TM, CB), lambda i: (0, i, 0)),
        ] + [_vmem_spec() for _ in range(7)],
        out_specs=_any_spec(),
        out_shape=jax.ShapeDtypeStruct(x.shape, F32),
        scratch_shapes=[
            pltpu.VMEM((2,) + buf_shape, F32), pltpu.SemaphoreType.DMA((2, n_copies)),
            pltpu.VMEM((2,) + buf_shape, F32), pltpu.SemaphoreType.DMA((2, n_copies)),
            pltpu.VMEM((TM, D_MODEL), BF16),
        ],
        compiler_params=pltpu.CompilerParams(
            dimension_semantics=("arbitrary",), vmem_limit_bytes=VMEM_LIMIT_BYTES),
        name="merge",
    )(x, ya, yb, *gw)


def _merge_prompt(x, ya, yb, gw, alpha):
    B, T, D = x.shape
    S = PROMPT_STEPS
    TM = S * SEG
    make_views = lambda hbm, buf: _prompt_views(hbm, buf, T // TM, TM, S, MERGE_STEPS)
    return _merge(x, ya, yb, gw, alpha, make_views, (MERGE_STEPS, SEG, D), SEG)


def _merge_sample(x, ya, yb, gw, alpha):
    D = D_MODEL
    S = x.shape[1] // D
    R = SAMPLE_ROWS
    make_views = lambda hbm, buf: _sample_views(hbm, buf, R, S)
    return _merge(x, ya, yb, gw, alpha, make_views, (S * R, D), S)


def _block_diag_pairs(w):
    nb = w.shape[0]
    w2 = w.reshape(nb // 2, 2, LRU_BLOCK_DIM, LRU_BLOCK_DIM)
    z = jnp.zeros_like(w2[:, 0])
    top = jnp.concatenate([w2[:, 0], z], axis=2)
    bot = jnp.concatenate([z, w2[:, 1]], axis=2)
    return jnp.concatenate([top, bot], axis=1).astype(BF16)


def _layer_weights(l, w_in, b_merge, pool_w, pool_scale, conv_w, conv_b, lru_wa, lru_ba,
                   lru_wx, lru_bx, lru_L, w_proj_a, w_proj_b, w_out, ln_g, ln_b):
    row = lambda v: v.reshape(1, -1)
    mixer = (
        w_in[l, :, :COL_M].astype(BF16),
        pool_w[l].astype(BF16), row(pool_scale[l]), conv_w[l], row(conv_b[l]),
        _block_diag_pairs(lru_wa[l]), _block_diag_pairs(lru_wx[l]),
        row(lru_ba[l]), row(lru_bx[l]), row(lru_L[l]),
    )
    assert len(mixer) == N_MIXER_WEIGHTS
    merge = (
        w_in[l, :, COL_M:].astype(BF16), b_merge[l],
        w_proj_a[l].astype(BF16), w_proj_b[l].astype(BF16), w_out[l].astype(BF16),
        row(ln_g[l]), row(ln_b[l]),
    )
    return mixer, merge


def kernel(x_prompt, x_sample, state_pool, state_conv, state_h, w_in, b_merge, pool_w, pool_scale, conv_w, conv_b, lru_wa, lru_ba, lru_wx, lru_bx, lru_L, w_proj_a, w_proj_b, w_out, ln_g, ln_b):
    depth = w_in.shape[0]
    alpha = (2.0 * depth) ** 0.25
    B = x_prompt.shape[0]
    Bs = x_sample.shape[0]

    xp, xs = x_prompt, x_sample.reshape(Bs, -1)
    pool_p, conv_p, h_p, pool_s, conv_s, h_s = [], [], [], [], [], []
    for l in range(depth):
        mw, gw = _layer_weights(l, w_in, b_merge, pool_w, pool_scale, conv_w, conv_b, lru_wa,
                                lru_ba, lru_wx, lru_bx, lru_L, w_proj_a, w_proj_b, w_out, ln_g, ln_b)
        ya, yb, pst, cst, hst = _mixer_prompt(xp, mw)
        xp = _merge_prompt(xp, ya, yb, gw, alpha)
        pool_p.append(pst.transpose(0, 2, 1, 3).reshape(B, POOL_BUF, POOL_WIDTH))
        conv_p.append(cst.transpose(0, 2, 1, 3).reshape(B, CONV_BUF, LRU_WIDTH))
        h_p.append(hst.transpose(0, 2, 1, 3).reshape(B, LRU_WIDTH))

        ya, yb, pst, cst, hst = _mixer_sample(
            xs, state_pool[l].reshape(Bs, -1), state_conv[l].reshape(Bs, -1), state_h[l], mw, PAST_LEN)
        xs = _merge_sample(xs, ya, yb, gw, alpha)
        pool_s.append(pst.reshape(state_pool.shape[1:]))
        conv_s.append(cst.reshape(state_conv.shape[1:]))
        h_s.append(hst)

    return (xp, xs.reshape(x_sample.shape),
            jnp.stack(pool_p), jnp.stack(conv_p), jnp.stack(h_p),
            jnp.stack(pool_s), jnp.stack(conv_s), jnp.stack(h_s))
```

```python
import functools

import jax
import jax.numpy as jnp
from jax import lax
from jax.experimental import pallas as pl
from jax.experimental.pallas import tpu as pltpu

D_MODEL = 2048
POOL_WIDTH = D_MODEL // 2
POOL_WINDOWS = (2, 4, 8, 16)
POOL_BUF = max(POOL_WINDOWS) - 1
LRU_WIDTH = D_MODEL
LRU_BLOCK_DIM = 128
CONV_WIDTH = 4
CONV_BUF = CONV_WIDTH - 1
LRU_C = 8.0
LN_EPS = 1e-5
PAST_LEN = 16384

COL_GA = POOL_WIDTH
COL_UB = 2 * POOL_WIDTH
COL_GB = COL_UB + LRU_WIDTH
COL_M = COL_GB + LRU_WIDTH

CB = 256
N_PB = POOL_WIDTH // CB
N_LB = LRU_WIDTH // CB
SEG = 8
PROMPT_STEPS = 64
SAMPLE_ROWS = 32
MERGE_STEPS = 32
MERGE_COLS = 512
WCOLS = 512
WROWS = 1024
VMEM_LIMIT_BYTES = 60 * 1024 * 1024

F32 = jnp.float32
BF16 = jnp.bfloat16


def _silu(v):
    return v * jax.nn.sigmoid(v)


def _dot(a, b):
    return jnp.dot(a, b, preferred_element_type=F32)


def _copies(views, sems, slot):
    return [pltpu.make_async_copy(src, dst, sems.at[slot, n]) for n, (src, dst) in enumerate(views)]


def _stream_in(i, n, views_fn, sems):
    slot = lax.rem(i, 2)

    @pl.when(i == 0)
    def _():
        for c in _copies(views_fn(i, slot), sems, slot):
            c.start()

    @pl.when(i + 1 < n)
    def _():
        for c in _copies(views_fn(i + 1, 1 - slot), sems, 1 - slot):
            c.start()

    for c in _copies(views_fn(i, slot), sems, slot):
        c.wait()
    return slot


def _stream_out(i, n, views_fn, sems):
    slot = lax.rem(i, 2)
    for c in _copies(views_fn(i, slot), sems, slot):
        c.start()

    @pl.when(i >= 1)
    def _():
        for c in _copies(views_fn(i - 1, 1 - slot), sems, 1 - slot):
            c.wait()

    @pl.when(i == n - 1)
    def _():
        for c in _copies(views_fn(i, slot), sems, slot):
            c.wait()


def _convert_weights(src_fn, n_col, n_rh, dst, stage, sem):
    n = n_col * n_rh

    def copy(q, slot):
        return pltpu.make_async_copy(src_fn(lax.rem(q, n_rh), lax.div(q, n_rh)), stage.at[slot], sem.at[slot])

    copy(0, 0).start()

    def body(q, carry):
        slot = lax.rem(q, 2)

        @pl.when(q + 1 < n)
        def _():
            copy(q + 1, 1 - slot).start()

        copy(q, slot).wait()
        rows = pl.ds(pl.multiple_of(lax.rem(q, n_rh) * WROWS, WROWS), WROWS)
        dst[lax.div(q, n_rh), rows, :] = stage[slot].astype(BF16)
        return carry

    lax.fori_loop(0, n, body, 0)


def _weight_window(w_hbm, l, col0):
    return lambda rh, c: w_hbm.at[l, pl.ds(rh * WROWS, WROWS), pl.ds(col0 + c * WCOLS, WCOLS)]


def _prompt_views(hbm, buf, nT, TM, S, steps):
    per = S // steps

    def views(i, slot, to_hbm=False):
        m = i // per
        b = m // nT
        row0 = (m % nT) * TM + (i % per) * steps
        pairs = [(hbm.at[b, pl.ds(row0 + s * S, steps), :], buf.at[slot, :, s, :]) for s in range(SEG)]
        return [(d, h) for h, d in pairs] if to_hbm else pairs

    return views


def _sample_views(hbm, buf, R, n_steps):
    C = buf.shape[-1]

    def views(i, slot, to_hbm=False):
        pairs = [(hbm.at[pl.ds(i * R, R), pl.ds(k * C, C)], buf.at[slot, pl.ds(k * R, R), :])
                 for k in range(n_steps)]
        return [(d, h) for h, d in pairs] if to_hbm else pairs

    return views


def _inv_count(rows, R, S, window, pos0, chained):
    assert R & (R - 1) == 0
    i = lax.broadcasted_iota(jnp.int32, (rows, 1), 0)
    step, seg = lax.shift_right_logical(i, R.bit_length() - 1), lax.bitwise_and(i, R - 1)
    pos = pos0 + (seg * S + step if chained else step)
    return 1.0 / jnp.minimum(pos + 1, window).astype(F32)


def _block_cols(k):
    if k < N_PB:
        return k * CB, COL_GA + k * CB, k * CB
    c = (k - N_PB) * CB
    return COL_UB + c, COL_GB + c, c


def _produce(k, xb_s, wmix, e_s, g_s, R, S):
    cu, cg, _ = _block_cols(k)
    xb = xb_s[...]
    e_s[pl.ds(POOL_BUF * R, S * R), :] = _dot(xb, wmix[cu // WCOLS, :, cu % WCOLS:cu % WCOLS + CB])
    g_s[...] = _dot(xb, wmix[cg // WCOLS, :, cg % WCOLS:cg % WCOLS + CB])


def _pool_mix(g, R, S, e_s, g_s, poolw, pscale, inv_cnt, ya_ref):
    TM = S * R
    H = POOL_BUF * R
    u = e_s[pl.ds(H, TM), :]
    acc = u
    for k in range(1, POOL_WINDOWS[g]):
        acc = acc + e_s[pl.ds(H - k * R, TM), :]
    diff = acc * inv_cnt - u
    mixed = _dot(diff.astype(BF16), poolw[g]) * pscale[:, g * CB:(g + 1) * CB]
    ya_ref[g] = (mixed * _silu(g_s[...])).astype(BF16)


def _lru_gates(blk, R, S, e_s, convw, convb, wabd, wxbd, ba, bx, lL, a_s, b_s):
    TM = S * R
    H0 = (POOL_BUF - CONV_BUF) * R
    cs = slice(blk * CB, (blk + 1) * CB)
    xc = convb[:, cs] + e_s[pl.ds(H0, TM), :] * convw[0:1, cs]
    for k in range(1, CONV_WIDTH):
        xc = xc + e_s[pl.ds(H0 + k * R, TM), :] * convw[k:k + 1, cs]
    xcb = xc.astype(BF16)
    r = jax.nn.sigmoid(_dot(xcb, wabd[blk]) + ba[:, cs])
    i = jax.nn.sigmoid(_dot(xcb, wxbd[blk]) + bx[:, cs])
    nsp = -LRU_C * jax.nn.softplus(-lL[:, cs])
    log_a = r * nsp
    a = jnp.exp(log_a)
    a_s[...] = a
    b_s[...] = jnp.sqrt((1.0 - a) * (1.0 + a)) * (i * xc)


def _halo_from_tail(tail, carry, n):
    t3 = tail.reshape(n, SEG, CB)
    sub = lax.broadcasted_iota(jnp.int32, (n, SEG, CB), 1)
    return jnp.where(sub == 0, carry.reshape(n, SEG, CB), pltpu.roll(t3, 1, 1)).reshape(n * SEG, CB)


def _last_seg_rows(tail, n):
    return tail.reshape(n, SEG, CB)[:, SEG - 1, :]


def _bcast_last_seg(tail, n):
    t3 = tail.reshape(n, SEG, CB)
    return jnp.broadcast_to(t3[:, SEG - 1:SEG, :], (n, SEG, CB)).reshape(n * SEG, CB)


def _run_pipeline(produce, pool_item, lru_item, slots):
    n = N_PB + N_LB
    produce(0, slots[0])
    for k in range(n):
        if k + 1 < n:
            produce(k + 1, slots[(k + 1) % 2])
        if k < N_PB:
            pool_item(k, slots[k % 2])
        else:
            lru_item(k - N_PB, slots[k % 2])


def _mixer_prompt_kernel(S, nT, n_tiles, layer, x_hbm, w_in_hbm, poolw, pscale, convw, convb, wabd, wxbd,
                         ba, bx, lL,
                         ya_ref, yb_ref, pst_ref, cst_ref, hst_ref, wmix,
                         xbuf, xsem, wstage, wsem, xb_s, e0_s, e1_s, g0_s, g1_s, a0_s, a1_s, b0_s, b1_s,
                         pc_s, cc_s, hc_s):
    R = SEG
    TM = S * R
    H = POOL_BUF * R
    i = pl.program_id(0)
    t = lax.rem(i, nT)

    @pl.when(i == 0)
    def _():
        _convert_weights(_weight_window(w_in_hbm, layer, 0), COL_M // WCOLS, D_MODEL // WROWS,
                         wmix, wstage, wsem)

    slot = _stream_in(i, n_tiles, _prompt_views(x_hbm, xbuf, nT, TM, S, S), xsem)

    @pl.when(t == 0)
    def _():
        pc_s[...] = jnp.zeros_like(pc_s)
        cc_s[...] = jnp.zeros_like(cc_s)
        hc_s[...] = jnp.zeros_like(hc_s)

    xb_s[...] = xbuf[slot].reshape(TM, D_MODEL).astype(BF16)
    sub = lax.broadcasted_iota(jnp.int32, (R, CB), 0)

    def produce(k, sl):
        _produce(k, xb_s, wmix, sl[0], sl[1], R, S)

    def pool_item(g, sl):
        e_s, g_s = sl[0], sl[1]
        n = POOL_BUF
        tail = e_s[pl.ds(H + TM - n * R, n * R), :]
        e_s[pl.ds(0, n * R), :] = _halo_from_tail(tail, pc_s[g], n)
        pc_s[g] = _bcast_last_seg(tail, n)
        pst_ref[0, g] = _last_seg_rows(tail, n)
        inv_cnt = _inv_count(TM, R, S, POOL_WINDOWS[g], t * TM, True)
        _pool_mix(g, R, S, e_s, g_s, poolw, pscale, inv_cnt, ya_ref)

    def lru_item(blk, sl):
        e_s, g_s, a_s, b_s = sl
        n = CONV_BUF
        tail = e_s[pl.ds(H + TM - n * R, n * R), :]
        e_s[pl.ds(H - n * R, n * R), :] = _halo_from_tail(tail, cc_s[blk], n)
        cc_s[blk] = _bcast_last_seg(tail, n)
        cst_ref[0, blk] = _last_seg_rows(tail, n)
        _lru_gates(blk, R, S, e_s, convw, convb, wabd, wxbd, ba, bx, lL, a_s, b_s)

        h_end = b_s[pl.ds(0, R), :]
        p_end = a_s[pl.ds(0, R), :]
        for j in range(1, S):
            rows = pl.ds(j * R, R)
            a = a_s[rows, :]
            h_end = a * h_end + b_s[rows, :]
            p_end = a * p_end
            b_s[rows, :] = h_end
            a_s[rows, :] = p_end

        cin = hc_s[blk]
        c = cin
        for s in range(1, R + 1):
            cin = (jnp.broadcast_to(p_end[s - 1:s, :], (R, CB)) * cin
                   + jnp.broadcast_to(h_end[s - 1:s, :], (R, CB)))
            if s < R:
                c = jnp.where(sub == s, cin, c)
        hc_s[blk] = cin
        hst_ref[0, blk] = cin[0:1, :]

        h = b_s[...].reshape(S, R, CB) + a_s[...].reshape(S, R, CB) * c[None]
        yb_ref[blk] = (h.reshape(TM, CB) * _silu(g_s[...])).astype(BF16)

    _run_pipeline(produce, pool_item, lru_item,
                  ((e0_s, g0_s, a0_s, b0_s), (e1_s, g1_s, a1_s, b1_s)))


def _mixer_sample_kernel(S, R, pos0, n_tiles, x_hbm, pin_hbm, cin_hbm, hin_ref,
                         wmix, poolw, pscale, convw, convb, wabd, wxbd, ba, bx, lL,
                         ya_ref, yb_ref, pout_hbm, cout_hbm, hst_ref,
                         xbuf, xsem, pbuf, psem, cbuf, csem, pobuf, posem, cobuf, cosem,
                         xb_s, e0_s, e1_s, g0_s, g1_s, a0_s, a1_s, b0_s, b1_s):
    TM = S * R
    H = POOL_BUF * R
    i = pl.program_id(0)
    keep_p = POOL_BUF - S
    assert keep_p > 0 and S >= CONV_BUF

    slot = _stream_in(i, n_tiles, _sample_views(x_hbm, xbuf, R, S), xsem)
    _stream_in(i, n_tiles, _sample_views(pin_hbm, pbuf, R, POOL_BUF), psem)
    _stream_in(i, n_tiles, _sample_views(cin_hbm, cbuf, R, CONV_BUF), csem)

    xb_s[...] = xbuf[slot].astype(BF16)
    pobuf[slot, pl.ds(0, keep_p * R), :] = pbuf[slot, pl.ds(S * R, keep_p * R), :]

    def produce(k, sl):
        _produce(k, xb_s, wmix, sl[0], sl[1], R, S)

    def pool_item(g, sl):
        e_s, g_s = sl[0], sl[1]
        cs = slice(g * CB, (g + 1) * CB)
        e_s[pl.ds(0, H), :] = pbuf[slot, :, cs]
        pobuf[slot, pl.ds(keep_p * R, TM), cs] = e_s[pl.ds(H, TM), :]
        inv_cnt = _inv_count(TM, R, S, POOL_WINDOWS[g], pos0, False)
        _pool_mix(g, R, S, e_s, g_s, poolw, pscale, inv_cnt, ya_ref)

    def lru_item(blk, sl):
        e_s, g_s, a_s, b_s = sl
        n = CONV_BUF
        cs = slice(blk * CB, (blk + 1) * CB)
        e_s[pl.ds(H - n * R, n * R), :] = cbuf[slot, :, cs]
        cobuf[slot, :, cs] = e_s[pl.ds(H + TM - n * R, n * R), :]
        _lru_gates(blk, R, S, e_s, convw, convb, wabd, wxbd, ba, bx, lL, a_s, b_s)

        h = hin_ref[:, cs]
        for j in range(S):
            rows = pl.ds(j * R, R)
            h = a_s[rows, :] * h + b_s[rows, :]
            b_s[rows, :] = h
        hst_ref[:, cs] = h
        yb_ref[blk] = (b_s[...] * _silu(g_s[...])).astype(BF16)

    _run_pipeline(produce, pool_item, lru_item,
                  ((e0_s, g0_s, a0_s, b0_s), (e1_s, g1_s, a1_s, b1_s)))

    pviews = _sample_views(pout_hbm, pobuf, R, POOL_BUF)
    cviews = _sample_views(cout_hbm, cobuf, R, CONV_BUF)
    _stream_out(i, n_tiles, functools.partial(pviews, to_hbm=True), posem)
    _stream_out(i, n_tiles, functools.partial(cviews, to_hbm=True), cosem)


def _slot_scratch(R, S):
    TM = S * R
    e = pltpu.VMEM(((POOL_BUF + S) * R, CB), F32)
    t = pltpu.VMEM((TM, CB), F32)
    return [e, e, t, t, t, t, t, t]


def _vmem_spec():
    return pl.BlockSpec(memory_space=pltpu.VMEM)


def _any_spec():
    return pl.BlockSpec(memory_space=pl.ANY)


N_MIXER_PARAMS = 9


def _mixer_prompt(x, w_in, layer, mw):
    B, T, D = x.shape
    S = PROMPT_STEPS
    TM = S * SEG
    nT = T // TM
    n_tiles = B * nT
    N = B * T
    out_shape = (
        jax.ShapeDtypeStruct((N_PB, N, CB), BF16),
        jax.ShapeDtypeStruct((N_LB, N, CB), BF16),
        jax.ShapeDtypeStruct((B, N_PB, POOL_BUF, CB), F32),
        jax.ShapeDtypeStruct((B, N_LB, CONV_BUF, CB), F32),
        jax.ShapeDtypeStruct((B, N_LB, 1, CB), F32),
        jax.ShapeDtypeStruct((COL_M // WCOLS, D, WCOLS), BF16),
    )
    out_specs = (
        pl.BlockSpec((N_PB, TM, CB), lambda i: (0, i, 0)),
        pl.BlockSpec((N_LB, TM, CB), lambda i: (0, i, 0)),
        pl.BlockSpec((1, N_PB, POOL_BUF, CB), lambda i: (i // nT, 0, 0, 0)),
        pl.BlockSpec((1, N_LB, CONV_BUF, CB), lambda i: (i // nT, 0, 0, 0)),
        pl.BlockSpec((1, N_LB, 1, CB), lambda i: (i // nT, 0, 0, 0)),
        _vmem_spec(),
    )
    scratch = [
        pltpu.VMEM((2, S, SEG, D), F32), pltpu.SemaphoreType.DMA((2, SEG)),
        pltpu.VMEM((2, WROWS, WCOLS), F32), pltpu.SemaphoreType.DMA((2,)),
        pltpu.VMEM((TM, D), BF16),
    ] + _slot_scratch(SEG, S) + [
        pltpu.VMEM((N_PB, POOL_BUF * SEG, CB), F32),
        pltpu.VMEM((N_LB, CONV_BUF * SEG, CB), F32),
        pltpu.VMEM((N_LB, SEG, CB), F32),
    ]
    return pl.pallas_call(
        functools.partial(_mixer_prompt_kernel, S, nT, n_tiles, layer),
        grid=(n_tiles,),
        in_specs=[_any_spec(), _any_spec()] + [_vmem_spec() for _ in range(N_MIXER_PARAMS)],
        out_specs=out_specs,
        out_shape=out_shape,
        scratch_shapes=scratch,
        compiler_params=pltpu.CompilerParams(
            dimension_semantics=("arbitrary",), vmem_limit_bytes=VMEM_LIMIT_BYTES),
        name="mixer_prompt",
    )(x, w_in, *mw)


def _mixer_sample(x, spool, sconv, sh, wmix, mw, pos0):
    Bs = x.shape[0]
    D = D_MODEL
    S = x.shape[1] // D
    R = SAMPLE_ROWS
    TM = S * R
    n_tiles = Bs // R
    N = Bs * S
    out_shape = (
        jax.ShapeDtypeStruct((N_PB, N, CB), BF16),
        jax.ShapeDtypeStruct((N_LB, N, CB), BF16),
        jax.ShapeDtypeStruct(spool.shape, F32),
        jax.ShapeDtypeStruct(sconv.shape, F32),
        jax.ShapeDtypeStruct(sh.shape, F32),
    )
    h_spec = pl.BlockSpec((R, LRU_WIDTH), lambda i: (i, 0))
    out_specs = (
        pl.BlockSpec((N_PB, TM, CB), lambda i: (0, i, 0)),
        pl.BlockSpec((N_LB, TM, CB), lambda i: (0, i, 0)),
        _any_spec(), _any_spec(), h_spec,
    )
    scratch = [
        pltpu.VMEM((2, TM, D), F32), pltpu.SemaphoreType.DMA((2, S)),
        pltpu.VMEM((2, POOL_BUF * R, POOL_WIDTH), F32), pltpu.SemaphoreType.DMA((2, POOL_BUF)),
        pltpu.VMEM((2, CONV_BUF * R, LRU_WIDTH), F32), pltpu.SemaphoreType.DMA((2, CONV_BUF)),
        pltpu.VMEM((2, POOL_BUF * R, POOL_WIDTH), F32), pltpu.SemaphoreType.DMA((2, POOL_BUF)),
        pltpu.VMEM((2, CONV_BUF * R, LRU_WIDTH), F32), pltpu.SemaphoreType.DMA((2, CONV_BUF)),
        pltpu.VMEM((TM, D), BF16),
    ] + _slot_scratch(R, S)
    return pl.pallas_call(
        functools.partial(_mixer_sample_kernel, S, R, pos0, n_tiles),
        grid=(n_tiles,),
        in_specs=[_any_spec(), _any_spec(), _any_spec(), h_spec]
        + [_vmem_spec() for _ in range(1 + N_MIXER_PARAMS)],
        out_specs=out_specs,
        out_shape=out_shape,
        scratch_shapes=scratch,
        compiler_params=pltpu.CompilerParams(
            dimension_semantics=("arbitrary",), vmem_limit_bytes=VMEM_LIMIT_BYTES),
        name="mixer_sample",
    )(x, spool, sconv, sh, wmix, *mw)


def _merge_kernel(alpha, n_tiles, make_views, layer, *refs):
    if layer is None:
        (x_hbm, ya_ref, yb_ref, wm, wpa, wpb, wout, bm, lng, lnb,
         o_hbm, xbuf, xsem, obuf, osem, m_s) = refs
    else:
        (x_hbm, ya_ref, yb_ref, w_in_hbm, wpa_hbm, wpb_hbm, wout_hbm, bm, lng, lnb,
         o_hbm, wm, wpa, wpb, wout, xbuf, xsem, obuf, osem, m_s, wstage, wsem) = refs
    i = pl.program_id(0)
    TM = m_s.shape[0]

    if layer is not None:
        @pl.when(i == 0)
        def _():
            nc = D_MODEL // WCOLS
            _convert_weights(_weight_window(w_in_hbm, layer, COL_M), 2 * nc, D_MODEL // WROWS, wm, wstage, wsem)
            _convert_weights(_weight_window(wpa_hbm, layer, 0), nc, POOL_WIDTH // WROWS, wpa, wstage, wsem)
            _convert_weights(_weight_window(wpb_hbm, layer, 0), nc, LRU_WIDTH // WROWS, wpb, wstage, wsem)
            _convert_weights(_weight_window(wout_hbm, layer, 0), nc, D_MODEL // WROWS, wout, wstage, wsem)

    slot = _stream_in(i, n_tiles, make_views(x_hbm, xbuf), xsem)
    xf = xbuf[slot].reshape(TM, D_MODEL)
    xb = xf.astype(BF16)
    for j in range(D_MODEL // MERGE_COLS):
        cs = slice(j * MERGE_COLS, (j + 1) * MERGE_COLS)
        pa = _dot(ya_ref[0], wpa[j, 0:CB, :])
        for g in range(1, N_PB):
            pa = pa + _dot(ya_ref[g], wpa[j, g * CB:(g + 1) * CB, :])
        pb = _dot(yb_ref[0], wpb[j, 0:CB, :])
        for g in range(1, N_LB):
            pb = pb + _dot(yb_ref[g], wpb[j, g * CB:(g + 1) * CB, :])
        g_a = jax.nn.sigmoid(_dot(xb, wm[j]) + bm[0:1, cs])
        g_b = jax.nn.sigmoid(_dot(xb, wm[D_MODEL // MERGE_COLS + j]) + bm[1:2, cs])
        m_s[:, cs] = (g_a * pa + g_b * pb).astype(BF16)
    mb = m_s[...]
    out = jnp.concatenate([_dot(mb, wout[j]) for j in range(D_MODEL // WCOLS)], axis=1)
    z = alpha * xf + out
    mu = jnp.mean(z, axis=-1, keepdims=True)
    zc = z - mu
    var = jnp.mean(zc * zc, axis=-1, keepdims=True)
    y = zc * lax.rsqrt(var + LN_EPS) * lng[...] + lnb[...]
    obuf[slot] = y.reshape(obuf.shape[1:])
    _stream_out(i, n_tiles, functools.partial(make_views(o_hbm, obuf), to_hbm=True), osem)


def _merge(x, ya, yb, weights, params, alpha, make_views, buf_shape, n_copies, layer=None):
    N = ya.shape[1]
    TM = buf_shape[0] * (buf_shape[1] if len(buf_shape) == 3 else 1)
    n_tiles = N // TM
    convert = layer is not None
    nc = D_MODEL // WCOLS
    w_shapes = [
        jax.ShapeDtypeStruct((2 * nc, D_MODEL, WCOLS), BF16),
        jax.ShapeDtypeStruct((nc, POOL_WIDTH, WCOLS), BF16),
        jax.ShapeDtypeStruct((nc, LRU_WIDTH, WCOLS), BF16),
        jax.ShapeDtypeStruct((nc, D_MODEL, WCOLS), BF16),
    ]
    w_spec = _any_spec if convert else _vmem_spec
    scratch = [
        pltpu.VMEM((2,) + buf_shape, F32), pltpu.SemaphoreType.DMA((2, n_copies)),
        pltpu.VMEM((2,) + buf_shape, F32), pltpu.SemaphoreType.DMA((2, n_copies)),
        pltpu.VMEM((TM, D_MODEL), BF16),
    ]
    if convert:
        scratch += [pltpu.VMEM((2, WROWS, WCOLS), F32), pltpu.SemaphoreType.DMA((2,))]
    out = pl.pallas_call(
        functools.partial(_merge_kernel, alpha, n_tiles, make_views, layer),
        grid=(n_tiles,),
        in_specs=[
            _any_spec(),
            pl.BlockSpec((N_PB, TM, CB), lambda i: (0, i, 0)),
            pl.BlockSpec((N_LB, TM, CB), lambda i: (0, i, 0)),
        ] + [w_spec() for _ in range(4)] + [_vmem_spec() for _ in range(3)],
        out_specs=[_any_spec()] + ([_vmem_spec() for _ in range(4)] if convert else []),
        out_shape=[jax.ShapeDtypeStruct(x.shape, F32)] + (w_shapes if convert else []),
        scratch_shapes=scratch,
        compiler_params=pltpu.CompilerParams(
            dimension_semantics=("arbitrary",), vmem_limit_bytes=VMEM_LIMIT_BYTES),
        name="merge",
    )(x, ya, yb, *weights, *params)
    return (out[0], tuple(out[1:])) if convert else out[0]


def _merge_prompt(x, ya, yb, weights_f32, layer, params, alpha):
    B, T, D = x.shape
    S = PROMPT_STEPS
    TM = S * SEG
    make_views = lambda hbm, buf: _prompt_views(hbm, buf, T // TM, TM, S, MERGE_STEPS)
    return _merge(x, ya, yb, weights_f32, params, alpha, make_views, (MERGE_STEPS, SEG, D), SEG, layer)


def _merge_sample(x, ya, yb, weights_bf16, params, alpha):
    D = D_MODEL
    S = x.shape[1] // D
    R = SAMPLE_ROWS
    make_views = lambda hbm, buf: _sample_views(hbm, buf, R, S)
    return _merge(x, ya, yb, weights_bf16, params, alpha, make_views, (S * R, D), S)


def _block_diag_pairs(w):
    nb = w.shape[0]
    w2 = w.reshape(nb // 2, 2, LRU_BLOCK_DIM, LRU_BLOCK_DIM)
    z = jnp.zeros_like(w2[:, 0])
    top = jnp.concatenate([w2[:, 0], z], axis=2)
    bot = jnp.concatenate([z, w2[:, 1]], axis=2)
    return jnp.concatenate([top, bot], axis=1).astype(BF16)


def _layer_params(l, b_merge, pool_w, pool_scale, conv_w, conv_b, lru_wa, lru_ba, lru_wx, lru_bx, lru_L,
                  ln_g, ln_b):
    row = lambda v: v.reshape(1, -1)
    mixer = (
        pool_w[l].astype(BF16), row(pool_scale[l]), conv_w[l], row(conv_b[l]),
        _block_diag_pairs(lru_wa[l]), _block_diag_pairs(lru_wx[l]),
        row(lru_ba[l]), row(lru_bx[l]), row(lru_L[l]),
    )
    assert len(mixer) == N_MIXER_PARAMS
    merge = (b_merge[l], row(ln_g[l]), row(ln_b[l]))
    return mixer, merge


def kernel(x_prompt, x_sample, state_pool, state_conv, state_h, w_in, b_merge, pool_w, pool_scale, conv_w, conv_b, lru_wa, lru_ba, lru_wx, lru_bx, lru_L, w_proj_a, w_proj_b, w_out, ln_g, ln_b):
    depth = w_in.shape[0]
    alpha = (2.0 * depth) ** 0.25
    B = x_prompt.shape[0]
    Bs = x_sample.shape[0]

    xp, xs = x_prompt, x_sample.reshape(Bs, -1)
    pool_p, conv_p, h_p, pool_s, conv_s, h_s = [], [], [], [], [], []
    for l in range(depth):
        mp, gp = _layer_params(l, b_merge, pool_w, pool_scale, conv_w, conv_b, lru_wa, lru_ba,
                               lru_wx, lru_bx, lru_L, ln_g, ln_b)
        ya, yb, pst, cst, hst, wmix = _mixer_prompt(xp, w_in, l, mp)
        xp, gw = _merge_prompt(xp, ya, yb, (w_in, w_proj_a, w_proj_b, w_out), l, gp, alpha)
        pool_p.append(pst.transpose(0, 2, 1, 3).reshape(B, POOL_BUF, POOL_WIDTH))
        conv_p.append(cst.transpose(0, 2, 1, 3).reshape(B, CONV_BUF, LRU_WIDTH))
        h_p.append(hst.transpose(0, 2, 1, 3).reshape(B, LRU_WIDTH))

        ya, yb, pst, cst, hst = _mixer_sample(
            xs, state_pool[l].reshape(Bs, -1), state_conv[l].reshape(Bs, -1), state_h[l], wmix, mp,
            PAST_LEN)
        xs = _merge_sample(xs, ya, yb, gw, gp, alpha)
        pool_s.append(pst.reshape(state_pool.shape[1:]))
        conv_s.append(cst.reshape(state_conv.shape[1:]))
        h_s.append(hst)

    return (xp, xs.reshape(x_sample.shape),
            jnp.stack(pool_p), jnp.stack(conv_p), jnp.stack(h_p),
            jnp.stack(pool_s), jnp.stack(conv_s), jnp.stack(h_s))
```

```python
import functools

import jax
import jax.numpy as jnp
from jax import lax
from jax.experimental import pallas as pl
from jax.experimental.pallas import tpu as pltpu

D_MODEL = 2048
POOL_WIDTH = D_MODEL // 2
POOL_WINDOWS = (2, 4, 8, 16)
POOL_BUF = max(POOL_WINDOWS) - 1
LRU_WIDTH = D_MODEL
LRU_BLOCK_DIM = 128
CONV_WIDTH = 4
CONV_BUF = CONV_WIDTH - 1
LRU_C = 8.0
LN_EPS = 1e-5
PAST_LEN = 16384

COL_GA = POOL_WIDTH
COL_UB = 2 * POOL_WIDTH
COL_GB = COL_UB + LRU_WIDTH
COL_M = COL_GB + LRU_WIDTH

CB = 256
N_PB = POOL_WIDTH // CB
N_LB = LRU_WIDTH // CB
SEG = 8
PROMPT_STEPS = 64
SAMPLE_ROWS = 32
MERGE_STEPS = 32
MERGE_COLS = 512
WCOLS = 512
WROWS = 1024
VMEM_LIMIT_BYTES = 60 * 1024 * 1024

F32 = jnp.float32
BF16 = jnp.bfloat16


def _silu(v):
    return v * jax.nn.sigmoid(v)


def _dot(a, b):
    return jnp.dot(a, b, preferred_element_type=F32)


def _copies(views, sems, slot):
    return [pltpu.make_async_copy(src, dst, sems.at[slot, n]) for n, (src, dst) in enumerate(views)]


def _tile_copies(views_fns, first_tiles, sems):
    bounds = list(first_tiles) + [None]

    def act(t, slot, start):
        def run(vf, t0):
            for c in _copies(vf(t - t0, slot), sems, slot):
                if start:
                    c.start()
                else:
                    c.wait()

        if len(views_fns) == 1:
            run(views_fns[0], 0)
            return
        for j, vf in enumerate(views_fns):
            cond = t >= bounds[j]
            if bounds[j + 1] is not None:
                cond = jnp.logical_and(cond, t < bounds[j + 1])
            pl.when(cond)(functools.partial(run, vf, bounds[j]))

    return act


def _stream_in(i, n, act):
    slot = lax.rem(i, 2)
    pl.when(i == 0)(lambda: act(i, slot, True))
    pl.when(i + 1 < n)(lambda: act(i + 1, 1 - slot, True))
    act(i, slot, False)
    return slot


def _stream_out(i, n, act):
    slot = lax.rem(i, 2)
    act(i, slot, True)
    pl.when(i >= 1)(lambda: act(i - 1, 1 - slot, False))
    pl.when(i == n - 1)(lambda: act(i, slot, False))


def _convert_weights(src_fn, n_col, n_rh, dst, stage, sem):
    n = n_col * n_rh

    def copy(q, slot):
        return pltpu.make_async_copy(src_fn(lax.rem(q, n_rh), lax.div(q, n_rh)), stage.at[slot], sem.at[slot])

    copy(0, 0).start()

    def body(q, carry):
        slot = lax.rem(q, 2)

        @pl.when(q + 1 < n)
        def _():
            copy(q + 1, 1 - slot).start()

        copy(q, slot).wait()
        rows = pl.ds(pl.multiple_of(lax.rem(q, n_rh) * WROWS, WROWS), WROWS)
        dst[lax.div(q, n_rh), rows, :] = stage[slot].astype(BF16)
        return carry

    lax.fori_loop(0, n, body, 0)


def _weight_window(w_hbm, l, col0):
    return lambda rh, c: w_hbm.at[l, pl.ds(rh * WROWS, WROWS), pl.ds(col0 + c * WCOLS, WCOLS)]


def _prompt_views(hbm, buf, nT, TM, S, steps):
    per = S // steps

    def views(i, slot, to_hbm=False):
        m = i // per
        b = m // nT
        row0 = (m % nT) * TM + (i % per) * steps
        pairs = [(hbm.at[b, pl.ds(row0 + s * S, steps), :], buf.at[slot, :, s, :]) for s in range(SEG)]
        return [(d, h) for h, d in pairs] if to_hbm else pairs

    return views


def _sample_views(hbm, buf, R, n_steps):
    C = buf.shape[-1]

    def views(i, slot, to_hbm=False):
        pairs = [(hbm.at[pl.ds(i * R, R), pl.ds(k * C, C)], buf.at[slot, pl.ds(k * R, R), :])
                 for k in range(n_steps)]
        return [(d, h) for h, d in pairs] if to_hbm else pairs

    return views


def _sample_x_views(hbm, buf, R, n_steps):
    G = R // SEG
    D = buf.shape[-1]

    def views(i, slot, to_hbm=False):
        pairs = [(hbm.at[pl.ds(i * G, G), :, pl.ds(k * D, D)], buf.at[slot, pl.ds(k * G, G), :, :])
                 for k in range(n_steps)]
        return [(d, h) for h, d in pairs] if to_hbm else pairs

    return views


def _inv_count(rows, R, S, window, pos0, chained):
    assert R & (R - 1) == 0
    i = lax.broadcasted_iota(jnp.int32, (rows, 1), 0)
    step, seg = lax.shift_right_logical(i, R.bit_length() - 1), lax.bitwise_and(i, R - 1)
    pos = pos0 + (seg * S + step if chained else step)
    return 1.0 / jnp.minimum(pos + 1, window).astype(F32)


def _block_cols(k):
    if k < N_PB:
        return k * CB, COL_GA + k * CB, k * CB
    c = (k - N_PB) * CB
    return COL_UB + c, COL_GB + c, c


def _produce(k, xb_s, wmix, e_s, g_s, R, S):
    cu, cg, _ = _block_cols(k)
    xb = xb_s[...]
    e_s[pl.ds(POOL_BUF * R, S * R), :] = _dot(xb, wmix[cu // WCOLS, :, cu % WCOLS:cu % WCOLS + CB])
    g_s[...] = _dot(xb, wmix[cg // WCOLS, :, cg % WCOLS:cg % WCOLS + CB])


def _pool_mix(g, R, S, e_s, g_s, poolw, pscale, inv_cnt, ya_ref):
    TM = S * R
    H = POOL_BUF * R
    u = e_s[pl.ds(H, TM), :]
    acc = u
    for k in range(1, POOL_WINDOWS[g]):
        acc = acc + e_s[pl.ds(H - k * R, TM), :]
    diff = acc * inv_cnt - u
    mixed = _dot(diff.astype(BF16), poolw[g]) * pscale[:, g * CB:(g + 1) * CB]
    ya_ref[g] = (mixed * _silu(g_s[...])).astype(BF16)


def _lru_gates(blk, R, S, e_s, convw, convb, wabd, wxbd, ba, bx, lL, a_s, b_s):
    TM = S * R
    H0 = (POOL_BUF - CONV_BUF) * R
    cs = slice(blk * CB, (blk + 1) * CB)
    xc = convb[:, cs] + e_s[pl.ds(H0, TM), :] * convw[0:1, cs]
    for k in range(1, CONV_WIDTH):
        xc = xc + e_s[pl.ds(H0 + k * R, TM), :] * convw[k:k + 1, cs]
    xcb = xc.astype(BF16)
    r = jax.nn.sigmoid(_dot(xcb, wabd[blk]) + ba[:, cs])
    i = jax.nn.sigmoid(_dot(xcb, wxbd[blk]) + bx[:, cs])
    nsp = -LRU_C * jax.nn.softplus(-lL[:, cs])
    log_a = r * nsp
    a = jnp.exp(log_a)
    a_s[...] = a
    b_s[...] = jnp.sqrt((1.0 - a) * (1.0 + a)) * (i * xc)


def _halo_from_tail(tail, carry, n):
    t3 = tail.reshape(n, SEG, CB)
    sub = lax.broadcasted_iota(jnp.int32, (n, SEG, CB), 1)
    return jnp.where(sub == 0, carry.reshape(n, SEG, CB), pltpu.roll(t3, 1, 1)).reshape(n * SEG, CB)


def _last_seg_rows(tail, n):
    return tail.reshape(n, SEG, CB)[:, SEG - 1, :]


def _bcast_last_seg(tail, n):
    t3 = tail.reshape(n, SEG, CB)
    return jnp.broadcast_to(t3[:, SEG - 1:SEG, :], (n, SEG, CB)).reshape(n * SEG, CB)


def _run_pipeline(produce, pool_item, lru_item, slots):
    n = N_PB + N_LB
    produce(0, slots[0])
    for k in range(n):
        if k + 1 < n:
            produce(k + 1, slots[(k + 1) % 2])
        if k < N_PB:
            pool_item(k, slots[k % 2])
        else:
            lru_item(k - N_PB, slots[k % 2])


def _mixer_prompt_kernel(S, nT, n_tiles, layer, x_hbm, w_in_hbm, poolw, pscale, convw, convb, wabd, wxbd,
                         ba, bx, lL, ya_all, yb_all,
                         ya_ref, yb_ref, pst_ref, cst_ref, hst_ref, wmix,
                         xbuf, xsem, wstage, wsem, xb_s, e0_s, e1_s, g0_s, g1_s, a0_s, a1_s, b0_s, b1_s,
                         pc_s, cc_s, hc_s):
    R = SEG
    TM = S * R
    H = POOL_BUF * R
    i = pl.program_id(0)
    t = lax.rem(i, nT)
    del ya_all, yb_all

    @pl.when(i == 0)
    def _():
        _convert_weights(_weight_window(w_in_hbm, layer, 0), COL_M // WCOLS, D_MODEL // WROWS,
                         wmix, wstage, wsem)

    slot = _stream_in(i, n_tiles, _tile_copies([_prompt_views(x_hbm, xbuf, nT, TM, S, S)], [0], xsem))

    @pl.when(t == 0)
    def _():
        pc_s[...] = jnp.zeros_like(pc_s)
        cc_s[...] = jnp.zeros_like(cc_s)
        hc_s[...] = jnp.zeros_like(hc_s)

    xb_s[slot] = xbuf[slot].reshape(TM, D_MODEL).astype(BF16)
    sub = lax.broadcasted_iota(jnp.int32, (R, CB), 0)

    def produce(k, sl):
        _produce(k, xb_s.at[slot], wmix, sl[0], sl[1], R, S)

    def pool_item(g, sl):
        e_s, g_s = sl[0], sl[1]
        n = POOL_BUF
        tail = e_s[pl.ds(H + TM - n * R, n * R), :]
        e_s[pl.ds(0, n * R), :] = _halo_from_tail(tail, pc_s[g], n)
        pc_s[g] = _bcast_last_seg(tail, n)
        pst_ref[0, g] = _last_seg_rows(tail, n)
        inv_cnt = _inv_count(TM, R, S, POOL_WINDOWS[g], t * TM, True)
        _pool_mix(g, R, S, e_s, g_s, poolw, pscale, inv_cnt, ya_ref)

    def lru_item(blk, sl):
        e_s, g_s, a_s, b_s = sl
        n = CONV_BUF
        tail = e_s[pl.ds(H + TM - n * R, n * R), :]
        e_s[pl.ds(H - n * R, n * R), :] = _halo_from_tail(tail, cc_s[blk], n)
        cc_s[blk] = _bcast_last_seg(tail, n)
        cst_ref[0, blk] = _last_seg_rows(tail, n)
        _lru_gates(blk, R, S, e_s, convw, convb, wabd, wxbd, ba, bx, lL, a_s, b_s)

        h_end = b_s[pl.ds(0, R), :]
        p_end = a_s[pl.ds(0, R), :]
        for j in range(1, S):
            rows = pl.ds(j * R, R)
            a = a_s[rows, :]
            h_end = a * h_end + b_s[rows, :]
            p_end = a * p_end
            b_s[rows, :] = h_end
            a_s[rows, :] = p_end

        cin = hc_s[blk]
        c = cin
        for s in range(1, R + 1):
            cin = (jnp.broadcast_to(p_end[s - 1:s, :], (R, CB)) * cin
                   + jnp.broadcast_to(h_end[s - 1:s, :], (R, CB)))
            if s < R:
                c = jnp.where(sub == s, cin, c)
        hc_s[blk] = cin
        hst_ref[0, blk] = cin[0:1, :]

        h = b_s[...].reshape(S, R, CB) + a_s[...].reshape(S, R, CB) * c[None]
        yb_ref[blk] = (h.reshape(TM, CB) * _silu(g_s[...])).astype(BF16)

    _run_pipeline(produce, pool_item, lru_item,
                  ((e0_s, g0_s, a0_s, b0_s), (e1_s, g1_s, a1_s, b1_s)))


def _mixer_sample_kernel(S, R, pos0, n_tiles, x_hbm, pin_hbm, cin_hbm, hin_ref,
                         wmix, poolw, pscale, convw, convb, wabd, wxbd, ba, bx, lL, ya_all, yb_all,
                         ya_ref, yb_ref, pout_hbm, cout_hbm, hst_ref,
                         xbuf, xsem, pbuf, psem, cbuf, csem, pobuf, posem, cobuf, cosem,
                         xb_s, e0_s, e1_s, g0_s, g1_s, a0_s, a1_s, b0_s, b1_s):
    TM = S * R
    H = POOL_BUF * R
    i = pl.program_id(0)
    keep_p = POOL_BUF - S
    assert keep_p > 0 and S >= CONV_BUF

    del ya_all, yb_all
    slot = _stream_in(i, n_tiles, _tile_copies([_sample_x_views(x_hbm, xbuf, R, S)], [0], xsem))
    _stream_in(i, n_tiles, _tile_copies([_sample_views(pin_hbm, pbuf, R, POOL_BUF)], [0], psem))
    _stream_in(i, n_tiles, _tile_copies([_sample_views(cin_hbm, cbuf, R, CONV_BUF)], [0], csem))

    xb_s[slot] = xbuf[slot].reshape(TM, D_MODEL).astype(BF16)
    pobuf[slot, pl.ds(0, keep_p * R), :] = pbuf[slot, pl.ds(S * R, keep_p * R), :]

    def produce(k, sl):
        _produce(k, xb_s.at[slot], wmix, sl[0], sl[1], R, S)

    def pool_item(g, sl):
        e_s, g_s = sl[0], sl[1]
        cs = slice(g * CB, (g + 1) * CB)
        e_s[pl.ds(0, H), :] = pbuf[slot, :, cs]
        pobuf[slot, pl.ds(keep_p * R, TM), cs] = e_s[pl.ds(H, TM), :]
        inv_cnt = _inv_count(TM, R, S, POOL_WINDOWS[g], pos0, False)
        _pool_mix(g, R, S, e_s, g_s, poolw, pscale, inv_cnt, ya_ref)

    def lru_item(blk, sl):
        e_s, g_s, a_s, b_s = sl
        n = CONV_BUF
        cs = slice(blk * CB, (blk + 1) * CB)
        e_s[pl.ds(H - n * R, n * R), :] = cbuf[slot, :, cs]
        cobuf[slot, :, cs] = e_s[pl.ds(H + TM - n * R, n * R), :]
        _lru_gates(blk, R, S, e_s, convw, convb, wabd, wxbd, ba, bx, lL, a_s, b_s)

        h = hin_ref[:, cs]
        for j in range(S):
            rows = pl.ds(j * R, R)
            h = a_s[rows, :] * h + b_s[rows, :]
            b_s[rows, :] = h
        hst_ref[:, cs] = h
        yb_ref[blk] = (b_s[...] * _silu(g_s[...])).astype(BF16)

    _run_pipeline(produce, pool_item, lru_item,
                  ((e0_s, g0_s, a0_s, b0_s), (e1_s, g1_s, a1_s, b1_s)))

    pviews = _sample_views(pout_hbm, pobuf, R, POOL_BUF)
    cviews = _sample_views(cout_hbm, cobuf, R, CONV_BUF)
    _stream_out(i, n_tiles, _tile_copies([functools.partial(pviews, to_hbm=True)], [0], posem))
    _stream_out(i, n_tiles, _tile_copies([functools.partial(cviews, to_hbm=True)], [0], cosem))


def _slot_scratch(R, S):
    TM = S * R
    e = pltpu.VMEM(((POOL_BUF + S) * R, CB), F32)
    t = pltpu.VMEM((TM, CB), F32)
    return [e, e, t, t, t, t, t, t]


def _vmem_spec():
    return pl.BlockSpec(memory_space=pltpu.VMEM)


def _any_spec():
    return pl.BlockSpec(memory_space=pl.ANY)


N_MIXER_PARAMS = 9


def _mixer_prompt(x, w_in, layer, mw, n_extra_rows):
    B, T, D = x.shape
    S = PROMPT_STEPS
    TM = S * SEG
    nT = T // TM
    n_tiles = B * nT
    N = B * T + n_extra_rows
    ya0 = jnp.zeros((N_PB, N, CB), BF16)
    yb0 = jnp.zeros((N_LB, N, CB), BF16)
    out_shape = (
        jax.ShapeDtypeStruct((N_PB, N, CB), BF16),
        jax.ShapeDtypeStruct((N_LB, N, CB), BF16),
        jax.ShapeDtypeStruct((B, N_PB, POOL_BUF, CB), F32),
        jax.ShapeDtypeStruct((B, N_LB, CONV_BUF, CB), F32),
        jax.ShapeDtypeStruct((B, N_LB, 1, CB), F32),
        jax.ShapeDtypeStruct((COL_M // WCOLS, D, WCOLS), BF16),
    )
    out_specs = (
        pl.BlockSpec((N_PB, TM, CB), lambda i: (0, i, 0)),
        pl.BlockSpec((N_LB, TM, CB), lambda i: (0, i, 0)),
        pl.BlockSpec((1, N_PB, POOL_BUF, CB), lambda i: (i // nT, 0, 0, 0)),
        pl.BlockSpec((1, N_LB, CONV_BUF, CB), lambda i: (i // nT, 0, 0, 0)),
        pl.BlockSpec((1, N_LB, 1, CB), lambda i: (i // nT, 0, 0, 0)),
        _vmem_spec(),
    )
    scratch = [
        pltpu.VMEM((2, S, SEG, D), F32), pltpu.SemaphoreType.DMA((2, SEG)),
        pltpu.VMEM((2, WROWS, WCOLS), F32), pltpu.SemaphoreType.DMA((2,)),
        pltpu.VMEM((2, TM, D), BF16),
    ] + _slot_scratch(SEG, S) + [
        pltpu.VMEM((N_PB, POOL_BUF * SEG, CB), F32),
        pltpu.VMEM((N_LB, CONV_BUF * SEG, CB), F32),
        pltpu.VMEM((N_LB, SEG, CB), F32),
    ]
    return pl.pallas_call(
        functools.partial(_mixer_prompt_kernel, S, nT, n_tiles, layer),
        grid=(n_tiles,),
        in_specs=[_any_spec(), _any_spec()] + [_vmem_spec() for _ in range(N_MIXER_PARAMS)]
        + [_any_spec(), _any_spec()],
        out_specs=out_specs,
        out_shape=out_shape,
        scratch_shapes=scratch,
        input_output_aliases={2 + N_MIXER_PARAMS: 0, 3 + N_MIXER_PARAMS: 1},
        compiler_params=pltpu.CompilerParams(
            dimension_semantics=("arbitrary",), vmem_limit_bytes=VMEM_LIMIT_BYTES),
        name="mixer_prompt",
    )(x, w_in, *mw, ya0, yb0)


def _mixer_sample(x, spool, sconv, sh, wmix, mw, ya_all, yb_all, pos0):
    Bs = sh.shape[0]
    D = D_MODEL
    S = x.shape[-1] // D
    R = SAMPLE_ROWS
    TM = S * R
    n_tiles = Bs // R
    first = (ya_all.shape[1] - Bs * S) // TM
    out_shape = (
        jax.ShapeDtypeStruct(ya_all.shape, BF16),
        jax.ShapeDtypeStruct(yb_all.shape, BF16),
        jax.ShapeDtypeStruct(spool.shape, F32),
        jax.ShapeDtypeStruct(sconv.shape, F32),
        jax.ShapeDtypeStruct(sh.shape, F32),
    )
    h_spec = pl.BlockSpec((R, LRU_WIDTH), lambda i: (i, 0))
    out_specs = (
        pl.BlockSpec((N_PB, TM, CB), lambda i: (0, first + i, 0)),
        pl.BlockSpec((N_LB, TM, CB), lambda i: (0, first + i, 0)),
        _any_spec(), _any_spec(), h_spec,
    )
    scratch = [
        pltpu.VMEM((2, TM // SEG, SEG, D), F32), pltpu.SemaphoreType.DMA((2, S)),
        pltpu.VMEM((2, POOL_BUF * R, POOL_WIDTH), F32), pltpu.SemaphoreType.DMA((2, POOL_BUF)),
        pltpu.VMEM((2, CONV_BUF * R, LRU_WIDTH), F32), pltpu.SemaphoreType.DMA((2, CONV_BUF)),
        pltpu.VMEM((2, POOL_BUF * R, POOL_WIDTH), F32), pltpu.SemaphoreType.DMA((2, POOL_BUF)),
        pltpu.VMEM((2, CONV_BUF * R, LRU_WIDTH), F32), pltpu.SemaphoreType.DMA((2, CONV_BUF)),
        pltpu.VMEM((2, TM, D), BF16),
    ] + _slot_scratch(R, S)
    return pl.pallas_call(
        functools.partial(_mixer_sample_kernel, S, R, pos0, n_tiles),
        grid=(n_tiles,),
        in_specs=[_any_spec(), _any_spec(), _any_spec(), h_spec]
        + [_vmem_spec() for _ in range(1 + N_MIXER_PARAMS)] + [_any_spec(), _any_spec()],
        out_specs=out_specs,
        out_shape=out_shape,
        scratch_shapes=scratch,
        input_output_aliases={5 + N_MIXER_PARAMS: 0, 6 + N_MIXER_PARAMS: 1},
        compiler_params=pltpu.CompilerParams(
            dimension_semantics=("arbitrary",), vmem_limit_bytes=VMEM_LIMIT_BYTES),
        name="mixer_sample",
    )(x, spool, sconv, sh, wmix, *mw, ya_all, yb_all)


def _merge_kernel(alpha, layer, n_prompt, n_tiles, S, nT, R, Ts,
                  xp_hbm, xs_hbm, ya_ref, yb_ref, w_in_hbm, wpa_hbm, wpb_hbm, wout_hbm, bm, lng, lnb,
                  op_hbm, os_hbm,
                  wm, wpa, wpb, wout, xbuf, xsem, obuf, osem, m_s, wstage, wsem):
    i = pl.program_id(0)
    TM = m_s.shape[0]

    @pl.when(i == 0)
    def _():
        nc = D_MODEL // WCOLS
        _convert_weights(_weight_window(w_in_hbm, layer, COL_M), 2 * nc, D_MODEL // WROWS, wm, wstage, wsem)
        _convert_weights(_weight_window(wpa_hbm, layer, 0), nc, POOL_WIDTH // WROWS, wpa, wstage, wsem)
        _convert_weights(_weight_window(wpb_hbm, layer, 0), nc, LRU_WIDTH // WROWS, wpb, wstage, wsem)
        _convert_weights(_weight_window(wout_hbm, layer, 0), nc, D_MODEL // WROWS, wout, wstage, wsem)

    def tile_views(p_hbm, s_hbm, buf, to_hbm):
        pv = _prompt_views(p_hbm, buf, nT, S * SEG, S, MERGE_STEPS)
        sv = _sample_x_views(s_hbm, buf, R, Ts)
        return [functools.partial(pv, to_hbm=to_hbm), functools.partial(sv, to_hbm=to_hbm)]

    slot = _stream_in(i, n_tiles, _tile_copies(tile_views(xp_hbm, xs_hbm, xbuf, False), [0, n_prompt], xsem))
    xf = xbuf[slot].reshape(TM, D_MODEL)
    xb = xf.astype(BF16)
    for j in range(D_MODEL // MERGE_COLS):
        cs = slice(j * MERGE_COLS, (j + 1) * MERGE_COLS)
        pa = _dot(ya_ref[0], wpa[j, 0:CB, :])
        for g in range(1, N_PB):
            pa = pa + _dot(ya_ref[g], wpa[j, g * CB:(g + 1) * CB, :])
        pb = _dot(yb_ref[0], wpb[j, 0:CB, :])
        for g in range(1, N_LB):
            pb = pb + _dot(yb_ref[g], wpb[j, g * CB:(g + 1) * CB, :])
        g_a = jax.nn.sigmoid(_dot(xb, wm[j]) + bm[0:1, cs])
        g_b = jax.nn.sigmoid(_dot(xb, wm[D_MODEL // MERGE_COLS + j]) + bm[1:2, cs])
        m_s[:, cs] = (g_a * pa + g_b * pb).astype(BF16)
    mb = m_s[...]
    out = jnp.concatenate([_dot(mb, wout[j]) for j in range(D_MODEL // WCOLS)], axis=1)
    z = alpha * xf + out
    mu = jnp.mean(z, axis=-1, keepdims=True)
    zc = z - mu
    var = jnp.mean(zc * zc, axis=-1, keepdims=True)
    y = zc * lax.rsqrt(var + LN_EPS) * lng[...] + lnb[...]
    obuf[slot] = y.reshape(obuf.shape[1:])
    _stream_out(i, n_tiles, _tile_copies(tile_views(op_hbm, os_hbm, obuf, True), [0, n_prompt], osem))


def _merge(xp, xs, ya, yb, weights_f32, layer, params, alpha):
    B, T, D = xp.shape
    S = PROMPT_STEPS
    Ts = xs.shape[-1] // D
    R = SAMPLE_ROWS
    TM = MERGE_STEPS * SEG
    assert Ts * R == TM
    n_prompt = B * T // TM
    n_tiles = ya.shape[1] // TM
    nc = D // WCOLS
    buf = (MERGE_STEPS, SEG, D)
    scratch = [
        pltpu.VMEM((2 * nc, D, WCOLS), BF16), pltpu.VMEM((nc, POOL_WIDTH, WCOLS), BF16),
        pltpu.VMEM((nc, LRU_WIDTH, WCOLS), BF16), pltpu.VMEM((nc, D, WCOLS), BF16),
        pltpu.VMEM((2,) + buf, F32), pltpu.SemaphoreType.DMA((2, SEG)),
        pltpu.VMEM((2,) + buf, F32), pltpu.SemaphoreType.DMA((2, SEG)),
        pltpu.VMEM((TM, D), BF16),
        pltpu.VMEM((2, WROWS, WCOLS), F32), pltpu.SemaphoreType.DMA((2,)),
    ]
    return pl.pallas_call(
        functools.partial(_merge_kernel, alpha, layer, n_prompt, n_tiles, S, T // (S * SEG), R, Ts),
        grid=(n_tiles,),
        in_specs=[
            _any_spec(), _any_spec(),
            pl.BlockSpec((N_PB, TM, CB), lambda i: (0, i, 0)),
            pl.BlockSpec((N_LB, TM, CB), lambda i: (0, i, 0)),
        ] + [_any_spec() for _ in range(4)] + [_vmem_spec() for _ in range(3)],
        out_specs=[_any_spec(), _any_spec()],
        out_shape=[jax.ShapeDtypeStruct(xp.shape, F32), jax.ShapeDtypeStruct(xs.shape, F32)],
        scratch_shapes=scratch,
        compiler_params=pltpu.CompilerParams(
            dimension_semantics=("arbitrary",), vmem_limit_bytes=VMEM_LIMIT_BYTES),
        name="merge",
    )(xp, xs, ya, yb, *weights_f32, *params)


def _block_diag_pairs(w):
    nb = w.shape[0]
    w2 = w.reshape(nb // 2, 2, LRU_BLOCK_DIM, LRU_BLOCK_DIM)
    z = jnp.zeros_like(w2[:, 0])
    top = jnp.concatenate([w2[:, 0], z], axis=2)
    bot = jnp.concatenate([z, w2[:, 1]], axis=2)
    return jnp.concatenate([top, bot], axis=1).astype(BF16)


def _layer_params(l, b_merge, pool_w, pool_scale, conv_w, conv_b, lru_wa, lru_ba, lru_wx, lru_bx, lru_L,
                  ln_g, ln_b):
    row = lambda v: v.reshape(1, -1)
    mixer = (
        pool_w[l].astype(BF16), row(pool_scale[l]), conv_w[l], row(conv_b[l]),
        _block_diag_pairs(lru_wa[l]), _block_diag_pairs(lru_wx[l]),
        row(lru_ba[l]), row(lru_bx[l]), row(lru_L[l]),
    )
    assert len(mixer) == N_MIXER_PARAMS
    merge = (b_merge[l], row(ln_g[l]), row(ln_b[l]))
    return mixer, merge


def kernel(x_prompt, x_sample, state_pool, state_conv, state_h, w_in, b_merge, pool_w, pool_scale, conv_w, conv_b, lru_wa, lru_ba, lru_wx, lru_bx, lru_L, w_proj_a, w_proj_b, w_out, ln_g, ln_b):
    depth = w_in.shape[0]
    alpha = (2.0 * depth) ** 0.25
    B = x_prompt.shape[0]
    Bs = x_sample.shape[0]

    Ts = x_sample.shape[1]
    xp, xs = x_prompt, x_sample.reshape(Bs // SEG, SEG, -1)
    pool_p, conv_p, h_p, pool_s, conv_s, h_s = [], [], [], [], [], []
    for l in range(depth):
        mp, gp = _layer_params(l, b_merge, pool_w, pool_scale, conv_w, conv_b, lru_wa, lru_ba,
                               lru_wx, lru_bx, lru_L, ln_g, ln_b)
        ya, yb, pst, cst, hst, wmix = _mixer_prompt(xp, w_in, l, mp, Bs * Ts)
        pool_p.append(pst.transpose(0, 2, 1, 3).reshape(B, POOL_BUF, POOL_WIDTH))
        conv_p.append(cst.transpose(0, 2, 1, 3).reshape(B, CONV_BUF, LRU_WIDTH))
        h_p.append(hst.transpose(0, 2, 1, 3).reshape(B, LRU_WIDTH))

        ya, yb, pst, cst, hst = _mixer_sample(
            xs, state_pool[l].reshape(Bs, -1), state_conv[l].reshape(Bs, -1), state_h[l], wmix, mp,
            ya, yb, PAST_LEN)
        xp, xs = _merge(xp, xs, ya, yb, (w_in, w_proj_a, w_proj_b, w_out), l, gp, alpha)
        pool_s.append(pst.reshape(state_pool.shape[1:]))
        conv_s.append(cst.reshape(state_conv.shape[1:]))
        h_s.append(hst)

    return (xp, xs.reshape(x_sample.shape),
            jnp.stack(pool_p), jnp.stack(conv_p), jnp.stack(h_p),
            jnp.stack(pool_s), jnp.stack(conv_s), jnp.stack(h_s))
```

```python
import functools

import jax
import jax.numpy as jnp
from jax import lax
from jax.experimental import pallas as pl
from jax.experimental.pallas import tpu as pltpu

D_MODEL = 2048
POOL_WIDTH = D_MODEL // 2
POOL_WINDOWS = (2, 4, 8, 16)
POOL_BUF = max(POOL_WINDOWS) - 1
LRU_WIDTH = D_MODEL
LRU_BLOCK_DIM = 128
CONV_WIDTH = 4
CONV_BUF = CONV_WIDTH - 1
LRU_C = 8.0
LN_EPS = 1e-5
PAST_LEN = 16384

COL_GA = POOL_WIDTH
COL_UB = 2 * POOL_WIDTH
COL_GB = COL_UB + LRU_WIDTH
COL_M = COL_GB + LRU_WIDTH

CB = 256
N_PB = POOL_WIDTH // CB
N_LB = LRU_WIDTH // CB
SEG = 8
PROMPT_STEPS = 64
SAMPLE_ROWS = 32
MERGE_STEPS = 32
MERGE_COLS = 512
WCOLS = 512
WROWS = 1024
VMEM_LIMIT_BYTES = 60 * 1024 * 1024

F32 = jnp.float32
BF16 = jnp.bfloat16


def _silu(v):
    return v * jax.nn.sigmoid(v)


def _dot(a, b):
    return jnp.dot(a, b, preferred_element_type=F32)


def _copies(views, sems, slot):
    return [pltpu.make_async_copy(src, dst, sems.at[slot, n]) for n, (src, dst) in enumerate(views)]


def _tile_copies(views_fns, first_tiles, sems):
    bounds = list(first_tiles) + [None]

    def act(t, slot, start):
        def run(vf, t0):
            for c in _copies(vf(t - t0, slot), sems, slot):
                if start:
                    c.start()
                else:
                    c.wait()

        if len(views_fns) == 1:
            run(views_fns[0], 0)
            return
        for j, vf in enumerate(views_fns):
            cond = t >= bounds[j]
            if bounds[j + 1] is not None:
                cond = jnp.logical_and(cond, t < bounds[j + 1])
            pl.when(cond)(functools.partial(run, vf, bounds[j]))

    return act


def _stream_in(i, n, act):
    slot = lax.rem(i, 2)
    pl.when(i == 0)(lambda: act(i, slot, True))
    pl.when(i + 1 < n)(lambda: act(i + 1, 1 - slot, True))
    act(i, slot, False)
    return slot


def _stream_out(i, n, act):
    slot = lax.rem(i, 2)
    act(i, slot, True)
    pl.when(i >= 1)(lambda: act(i - 1, 1 - slot, False))
    pl.when(i == n - 1)(lambda: act(i, slot, False))


def _convert_weights(src_fn, n_col, n_rh, dst, stage, sem):
    n = n_col * n_rh

    def copy(q, slot):
        return pltpu.make_async_copy(src_fn(lax.rem(q, n_rh), lax.div(q, n_rh)), stage.at[slot], sem.at[slot])

    copy(0, 0).start()

    def body(q, carry):
        slot = lax.rem(q, 2)

        @pl.when(q + 1 < n)
        def _():
            copy(q + 1, 1 - slot).start()

        copy(q, slot).wait()
        rows = pl.ds(pl.multiple_of(lax.rem(q, n_rh) * WROWS, WROWS), WROWS)
        dst[lax.div(q, n_rh), rows, :] = stage[slot].astype(BF16)
        return carry

    lax.fori_loop(0, n, body, 0)


def _weight_window(w_hbm, l, col0):
    return lambda rh, c: w_hbm.at[l, pl.ds(rh * WROWS, WROWS), pl.ds(col0 + c * WCOLS, WCOLS)]


def _prompt_views(hbm, buf, nT, TM, S, steps):
    per = S // steps

    def views(i, slot, to_hbm=False):
        m = i // per
        b = m // nT
        row0 = (m % nT) * TM + (i % per) * steps
        pairs = [(hbm.at[b, pl.ds(row0 + s * S, steps), :], buf.at[slot, :, s, :]) for s in range(SEG)]
        return [(d, h) for h, d in pairs] if to_hbm else pairs

    return views


def _sample_views(hbm, buf, R, n_steps):
    C = buf.shape[-1]

    def views(i, slot, to_hbm=False):
        pairs = [(hbm.at[pl.ds(i * R, R), pl.ds(k * C, C)], buf.at[slot, pl.ds(k * R, R), :])
                 for k in range(n_steps)]
        return [(d, h) for h, d in pairs] if to_hbm else pairs

    return views


def _sample_x_views(hbm, buf, R, n_steps):
    G = R // SEG
    D = buf.shape[-1]

    def views(i, slot, to_hbm=False):
        pairs = [(hbm.at[pl.ds(i * G, G), :, pl.ds(k * D, D)], buf.at[slot, pl.ds(k * G, G), :, :])
                 for k in range(n_steps)]
        return [(d, h) for h, d in pairs] if to_hbm else pairs

    return views


def _inv_count(rows, R, S, window, pos0, chained):
    assert R & (R - 1) == 0
    i = lax.broadcasted_iota(jnp.int32, (rows, 1), 0)
    step, seg = lax.shift_right_logical(i, R.bit_length() - 1), lax.bitwise_and(i, R - 1)
    pos = pos0 + (seg * S + step if chained else step)
    return 1.0 / jnp.minimum(pos + 1, window).astype(F32)


def _block_cols(k):
    if k < N_PB:
        return k * CB, COL_GA + k * CB, k * CB
    c = (k - N_PB) * CB
    return COL_UB + c, COL_GB + c, c


def _produce(k, xb_s, wmix, e_s, g_s, R, S):
    cu, cg, _ = _block_cols(k)
    xb = xb_s[...]
    e_s[pl.ds(POOL_BUF * R, S * R), :] = _dot(xb, wmix[cu // WCOLS, :, cu % WCOLS:cu % WCOLS + CB])
    g_s[...] = _dot(xb, wmix[cg // WCOLS, :, cg % WCOLS:cg % WCOLS + CB])


def _pool_mix(g, R, S, e_s, g_s, poolw, pscale, inv_cnt, ya_ref):
    TM = S * R
    H = POOL_BUF * R
    u = e_s[pl.ds(H, TM), :]
    acc = u
    for k in range(1, POOL_WINDOWS[g]):
        acc = acc + e_s[pl.ds(H - k * R, TM), :]
    diff = acc * inv_cnt - u
    mixed = _dot(diff.astype(BF16), poolw[g]) * pscale[:, g * CB:(g + 1) * CB]
    ya_ref[g] = (mixed * _silu(g_s[...])).astype(BF16)


def _lru_gates(blk, R, S, e_s, convw, convb, wabd, wxbd, ba, bx, lL, a_s, b_s):
    TM = S * R
    H0 = (POOL_BUF - CONV_BUF) * R
    cs = slice(blk * CB, (blk + 1) * CB)
    xc = convb[:, cs] + e_s[pl.ds(H0, TM), :] * convw[0:1, cs]
    for k in range(1, CONV_WIDTH):
        xc = xc + e_s[pl.ds(H0 + k * R, TM), :] * convw[k:k + 1, cs]
    xcb = xc.astype(BF16)
    r = jax.nn.sigmoid(_dot(xcb, wabd[blk]) + ba[:, cs])
    i = jax.nn.sigmoid(_dot(xcb, wxbd[blk]) + bx[:, cs])
    nsp = -LRU_C * jax.nn.softplus(-lL[:, cs])
    log_a = r * nsp
    a = jnp.exp(log_a)
    a_s[...] = a
    b_s[...] = jnp.sqrt((1.0 - a) * (1.0 + a)) * (i * xc)


def _halo_from_tail(tail, carry, n):
    t3 = tail.reshape(n, SEG, CB)
    sub = lax.broadcasted_iota(jnp.int32, (n, SEG, CB), 1)
    return jnp.where(sub == 0, carry.reshape(n, SEG, CB), pltpu.roll(t3, 1, 1)).reshape(n * SEG, CB)


def _last_seg_rows(tail, n):
    return tail.reshape(n, SEG, CB)[:, SEG - 1, :]


def _bcast_last_seg(tail, n):
    t3 = tail.reshape(n, SEG, CB)
    return jnp.broadcast_to(t3[:, SEG - 1:SEG, :], (n, SEG, CB)).reshape(n * SEG, CB)


def _run_pipeline(produce, pool_item, lru_item, slots):
    n = N_PB + N_LB
    produce(0, slots[0])
    for k in range(n):
        if k + 1 < n:
            produce(k + 1, slots[(k + 1) % 2])
        if k < N_PB:
            pool_item(k, slots[k % 2])
        else:
            lru_item(k - N_PB, slots[k % 2])


def _mixer_prompt_kernel(S, nT, n_tiles, layer, x_hbm, w_in_hbm, poolw, pscale, convw, convb, wabd, wxbd,
                         ba, bx, lL,
                         ya_ref, yb_ref, pst_ref, cst_ref, hst_ref, wmix,
                         xbuf, xsem, wstage, wsem, xb_s, e0_s, e1_s, g0_s, g1_s, a0_s, a1_s, b0_s, b1_s,
                         pc_s, cc_s, hc_s):
    R = SEG
    TM = S * R
    H = POOL_BUF * R
    i = pl.program_id(0)
    t = lax.rem(i, nT)

    @pl.when(i == 0)
    def _():
        _convert_weights(_weight_window(w_in_hbm, layer, 0), COL_M // WCOLS, D_MODEL // WROWS,
                         wmix, wstage, wsem)

    slot = _stream_in(i, n_tiles, _tile_copies([_prompt_views(x_hbm, xbuf, nT, TM, S, S)], [0], xsem))

    @pl.when(t == 0)
    def _():
        pc_s[...] = jnp.zeros_like(pc_s)
        cc_s[...] = jnp.zeros_like(cc_s)
        hc_s[...] = jnp.zeros_like(hc_s)

    xb_s[slot] = xbuf[slot].reshape(TM, D_MODEL).astype(BF16)
    sub = lax.broadcasted_iota(jnp.int32, (R, CB), 0)

    def produce(k, sl):
        _produce(k, xb_s.at[slot], wmix, sl[0], sl[1], R, S)

    def pool_item(g, sl):
        e_s, g_s = sl[0], sl[1]
        n = POOL_BUF
        tail = e_s[pl.ds(H + TM - n * R, n * R), :]
        e_s[pl.ds(0, n * R), :] = _halo_from_tail(tail, pc_s[g], n)
        pc_s[g] = _bcast_last_seg(tail, n)
        pst_ref[0, g] = _last_seg_rows(tail, n)
        inv_cnt = _inv_count(TM, R, S, POOL_WINDOWS[g], t * TM, True)
        _pool_mix(g, R, S, e_s, g_s, poolw, pscale, inv_cnt, ya_ref)

    def lru_item(blk, sl):
        e_s, g_s, a_s, b_s = sl
        n = CONV_BUF
        tail = e_s[pl.ds(H + TM - n * R, n * R), :]
        e_s[pl.ds(H - n * R, n * R), :] = _halo_from_tail(tail, cc_s[blk], n)
        cc_s[blk] = _bcast_last_seg(tail, n)
        cst_ref[0, blk] = _last_seg_rows(tail, n)
        _lru_gates(blk, R, S, e_s, convw, convb, wabd, wxbd, ba, bx, lL, a_s, b_s)

        h_end = b_s[pl.ds(0, R), :]
        p_end = a_s[pl.ds(0, R), :]
        for j in range(1, S):
            rows = pl.ds(j * R, R)
            a = a_s[rows, :]
            h_end = a * h_end + b_s[rows, :]
            p_end = a * p_end
            b_s[rows, :] = h_end
            a_s[rows, :] = p_end

        cin = hc_s[blk]
        c = cin
        for s in range(1, R + 1):
            cin = (jnp.broadcast_to(p_end[s - 1:s, :], (R, CB)) * cin
                   + jnp.broadcast_to(h_end[s - 1:s, :], (R, CB)))
            if s < R:
                c = jnp.where(sub == s, cin, c)
        hc_s[blk] = cin
        hst_ref[0, blk] = cin[0:1, :]

        h = b_s[...].reshape(S, R, CB) + a_s[...].reshape(S, R, CB) * c[None]
        yb_ref[blk] = (h.reshape(TM, CB) * _silu(g_s[...])).astype(BF16)

    _run_pipeline(produce, pool_item, lru_item,
                  ((e0_s, g0_s, a0_s, b0_s), (e1_s, g1_s, a1_s, b1_s)))


def _mixer_sample_kernel(S, R, pos0, n_tiles, x_hbm, pin_hbm, cin_hbm, hin_ref,
                         wmix, poolw, pscale, convw, convb, wabd, wxbd, ba, bx, lL,
                         ya_ref, yb_ref, pout_hbm, cout_hbm, hst_ref,
                         xbuf, xsem, pbuf, psem, cbuf, csem, pobuf, posem, cobuf, cosem,
                         xb_s, e0_s, e1_s, g0_s, g1_s, a0_s, a1_s, b0_s, b1_s):
    TM = S * R
    H = POOL_BUF * R
    i = pl.program_id(0)
    keep_p = POOL_BUF - S
    assert keep_p > 0 and S >= CONV_BUF

    slot = _stream_in(i, n_tiles, _tile_copies([_sample_x_views(x_hbm, xbuf, R, S)], [0], xsem))
    _stream_in(i, n_tiles, _tile_copies([_sample_views(pin_hbm, pbuf, R, POOL_BUF)], [0], psem))
    _stream_in(i, n_tiles, _tile_copies([_sample_views(cin_hbm, cbuf, R, CONV_BUF)], [0], csem))

    xb_s[slot] = xbuf[slot].reshape(TM, D_MODEL).astype(BF16)
    pobuf[slot, pl.ds(0, keep_p * R), :] = pbuf[slot, pl.ds(S * R, keep_p * R), :]

    def produce(k, sl):
        _produce(k, xb_s.at[slot], wmix, sl[0], sl[1], R, S)

    def pool_item(g, sl):
        e_s, g_s = sl[0], sl[1]
        cs = slice(g * CB, (g + 1) * CB)
        e_s[pl.ds(0, H), :] = pbuf[slot, :, cs]
        pobuf[slot, pl.ds(keep_p * R, TM), cs] = e_s[pl.ds(H, TM), :]
        inv_cnt = _inv_count(TM, R, S, POOL_WINDOWS[g], pos0, False)
        _pool_mix(g, R, S, e_s, g_s, poolw, pscale, inv_cnt, ya_ref)

    def lru_item(blk, sl):
        e_s, g_s, a_s, b_s = sl
        n = CONV_BUF
        cs = slice(blk * CB, (blk + 1) * CB)
        e_s[pl.ds(H - n * R, n * R), :] = cbuf[slot, :, cs]
        cobuf[slot, :, cs] = e_s[pl.ds(H + TM - n * R, n * R), :]
        _lru_gates(blk, R, S, e_s, convw, convb, wabd, wxbd, ba, bx, lL, a_s, b_s)

        h = hin_ref[:, cs]
        for j in range(S):
            rows = pl.ds(j * R, R)
            h = a_s[rows, :] * h + b_s[rows, :]
            b_s[rows, :] = h
        hst_ref[:, cs] = h
        yb_ref[blk] = (b_s[...] * _silu(g_s[...])).astype(BF16)

    _run_pipeline(produce, pool_item, lru_item,
                  ((e0_s, g0_s, a0_s, b0_s), (e1_s, g1_s, a1_s, b1_s)))

    pviews = _sample_views(pout_hbm, pobuf, R, POOL_BUF)
    cviews = _sample_views(cout_hbm, cobuf, R, CONV_BUF)
    _stream_out(i, n_tiles, _tile_copies([functools.partial(pviews, to_hbm=True)], [0], posem))
    _stream_out(i, n_tiles, _tile_copies([functools.partial(cviews, to_hbm=True)], [0], cosem))


def _slot_scratch(R, S):
    TM = S * R
    e = pltpu.VMEM(((POOL_BUF + S) * R, CB), F32)
    t = pltpu.VMEM((TM, CB), F32)
    return [e, e, t, t, t, t, t, t]


def _vmem_spec():
    return pl.BlockSpec(memory_space=pltpu.VMEM)


def _any_spec():
    return pl.BlockSpec(memory_space=pl.ANY)


N_MIXER_PARAMS = 9


def _mixer_prompt(x, w_in, layer, mw):
    B, T, D = x.shape
    S = PROMPT_STEPS
    TM = S * SEG
    nT = T // TM
    n_tiles = B * nT
    N = B * T
    out_shape = (
        jax.ShapeDtypeStruct((N_PB, N, CB), BF16),
        jax.ShapeDtypeStruct((N_LB, N, CB), BF16),
        jax.ShapeDtypeStruct((B, N_PB, POOL_BUF, CB), F32),
        jax.ShapeDtypeStruct((B, N_LB, CONV_BUF, CB), F32),
        jax.ShapeDtypeStruct((B, N_LB, 1, CB), F32),
        jax.ShapeDtypeStruct((COL_M // WCOLS, D, WCOLS), BF16),
    )
    out_specs = (
        pl.BlockSpec((N_PB, TM, CB), lambda i: (0, i, 0)),
        pl.BlockSpec((N_LB, TM, CB), lambda i: (0, i, 0)),
        pl.BlockSpec((1, N_PB, POOL_BUF, CB), lambda i: (i // nT, 0, 0, 0)),
        pl.BlockSpec((1, N_LB, CONV_BUF, CB), lambda i: (i // nT, 0, 0, 0)),
        pl.BlockSpec((1, N_LB, 1, CB), lambda i: (i // nT, 0, 0, 0)),
        _vmem_spec(),
    )
    scratch = [
        pltpu.VMEM((2, S, SEG, D), F32), pltpu.SemaphoreType.DMA((2, SEG)),
        pltpu.VMEM((2, WROWS, WCOLS), F32), pltpu.SemaphoreType.DMA((2,)),
        pltpu.VMEM((2, TM, D), BF16),
    ] + _slot_scratch(SEG, S) + [
        pltpu.VMEM((N_PB, POOL_BUF * SEG, CB), F32),
        pltpu.VMEM((N_LB, CONV_BUF * SEG, CB), F32),
        pltpu.VMEM((N_LB, SEG, CB), F32),
    ]
    return pl.pallas_call(
        functools.partial(_mixer_prompt_kernel, S, nT, n_tiles, layer),
        grid=(n_tiles,),
        in_specs=[_any_spec(), _any_spec()] + [_vmem_spec() for _ in range(N_MIXER_PARAMS)],
        out_specs=out_specs,
        out_shape=out_shape,
        scratch_shapes=scratch,
        compiler_params=pltpu.CompilerParams(
            dimension_semantics=("arbitrary",), vmem_limit_bytes=VMEM_LIMIT_BYTES),
        name="mixer_prompt",
    )(x, w_in, *mw)


def _mixer_sample(x, spool, sconv, sh, wmix, mw, pos0):
    Bs = sh.shape[0]
    D = D_MODEL
    S = x.shape[-1] // D
    R = SAMPLE_ROWS
    TM = S * R
    n_tiles = Bs // R
    out_shape = (
        jax.ShapeDtypeStruct((N_PB, Bs * S, CB), BF16),
        jax.ShapeDtypeStruct((N_LB, Bs * S, CB), BF16),
        jax.ShapeDtypeStruct(spool.shape, F32),
        jax.ShapeDtypeStruct(sconv.shape, F32),
        jax.ShapeDtypeStruct(sh.shape, F32),
    )
    h_spec = pl.BlockSpec((R, LRU_WIDTH), lambda i: (i, 0))
    out_specs = (
        pl.BlockSpec((N_PB, TM, CB), lambda i: (0, i, 0)),
        pl.BlockSpec((N_LB, TM, CB), lambda i: (0, i, 0)),
        _any_spec(), _any_spec(), h_spec,
    )
    scratch = [
        pltpu.VMEM((2, TM // SEG, SEG, D), F32), pltpu.SemaphoreType.DMA((2, S)),
        pltpu.VMEM((2, POOL_BUF * R, POOL_WIDTH), F32), pltpu.SemaphoreType.DMA((2, POOL_BUF)),
        pltpu.VMEM((2, CONV_BUF * R, LRU_WIDTH), F32), pltpu.SemaphoreType.DMA((2, CONV_BUF)),
        pltpu.VMEM((2, POOL_BUF * R, POOL_WIDTH), F32), pltpu.SemaphoreType.DMA((2, POOL_BUF)),
        pltpu.VMEM((2, CONV_BUF * R, LRU_WIDTH), F32), pltpu.SemaphoreType.DMA((2, CONV_BUF)),
        pltpu.VMEM((2, TM, D), BF16),
    ] + _slot_scratch(R, S)
    return pl.pallas_call(
        functools.partial(_mixer_sample_kernel, S, R, pos0, n_tiles),
        grid=(n_tiles,),
        in_specs=[_any_spec(), _any_spec(), _any_spec(), h_spec]
        + [_vmem_spec() for _ in range(1 + N_MIXER_PARAMS)],
        out_specs=out_specs,
        out_shape=out_shape,
        scratch_shapes=scratch,
        compiler_params=pltpu.CompilerParams(
            dimension_semantics=("arbitrary",), vmem_limit_bytes=VMEM_LIMIT_BYTES),
        name="mixer_sample",
    )(x, spool, sconv, sh, wmix, *mw)


def _merge_kernel(alpha, layer, n_prompt, n_tiles, S, nT, R, Ts,
                  xp_hbm, xs_hbm, yap_ref, ybp_ref, yas_ref, ybs_ref,
                  w_in_hbm, wpa_hbm, wpb_hbm, wout_hbm, bm, lng, lnb,
                  op_hbm, os_hbm,
                  wm, wpa, wpb, wout, xbuf, xsem, obuf, osem, ya_ref, yb_ref, m_s, wstage, wsem):
    i = pl.program_id(0)
    TM = m_s.shape[0]

    @pl.when(i == 0)
    def _():
        nc = D_MODEL // WCOLS
        _convert_weights(_weight_window(w_in_hbm, layer, COL_M), 2 * nc, D_MODEL // WROWS, wm, wstage, wsem)
        _convert_weights(_weight_window(wpa_hbm, layer, 0), nc, POOL_WIDTH // WROWS, wpa, wstage, wsem)
        _convert_weights(_weight_window(wpb_hbm, layer, 0), nc, LRU_WIDTH // WROWS, wpb, wstage, wsem)
        _convert_weights(_weight_window(wout_hbm, layer, 0), nc, D_MODEL // WROWS, wout, wstage, wsem)

    def tile_views(p_hbm, s_hbm, buf, to_hbm):
        pv = _prompt_views(p_hbm, buf, nT, S * SEG, S, MERGE_STEPS)
        sv = _sample_x_views(s_hbm, buf, R, Ts)
        return [functools.partial(pv, to_hbm=to_hbm), functools.partial(sv, to_hbm=to_hbm)]

    slot = _stream_in(i, n_tiles, _tile_copies(tile_views(xp_hbm, xs_hbm, xbuf, False), [0, n_prompt], xsem))
    xf = xbuf[slot].reshape(TM, D_MODEL)
    xb = xf.astype(BF16)
    is_prompt = i < n_prompt
    ya_ref[...] = jnp.where(is_prompt, yap_ref[...], yas_ref[...])
    yb_ref[...] = jnp.where(is_prompt, ybp_ref[...], ybs_ref[...])
    for j in range(D_MODEL // MERGE_COLS):
        cs = slice(j * MERGE_COLS, (j + 1) * MERGE_COLS)
        pa = _dot(ya_ref[0], wpa[j, 0:CB, :])
        for g in range(1, N_PB):
            pa = pa + _dot(ya_ref[g], wpa[j, g * CB:(g + 1) * CB, :])
        pb = _dot(yb_ref[0], wpb[j, 0:CB, :])
        for g in range(1, N_LB):
            pb = pb + _dot(yb_ref[g], wpb[j, g * CB:(g + 1) * CB, :])
        g_a = jax.nn.sigmoid(_dot(xb, wm[j]) + bm[0:1, cs])
        g_b = jax.nn.sigmoid(_dot(xb, wm[D_MODEL // MERGE_COLS + j]) + bm[1:2, cs])
        m_s[:, cs] = (g_a * pa + g_b * pb).astype(BF16)
    mb = m_s[...]
    out = jnp.concatenate([_dot(mb, wout[j]) for j in range(D_MODEL // WCOLS)], axis=1)
    z = alpha * xf + out
    mu = jnp.mean(z, axis=-1, keepdims=True)
    zc = z - mu
    var = jnp.mean(zc * zc, axis=-1, keepdims=True)
    y = zc * lax.rsqrt(var + LN_EPS) * lng[...] + lnb[...]
    obuf[slot] = y.reshape(obuf.shape[1:])
    _stream_out(i, n_tiles, _tile_copies(tile_views(op_hbm, os_hbm, obuf, True), [0, n_prompt], osem))


def _merge(xp, xs, yp, ys, weights_f32, layer, params, alpha):
    B, T, D = xp.shape
    S = PROMPT_STEPS
    Ts = xs.shape[-1] // D
    R = SAMPLE_ROWS
    TM = MERGE_STEPS * SEG
    assert Ts * R == TM
    n_prompt = B * T // TM
    n_tiles = n_prompt + ys[0].shape[1] // TM
    nc = D // WCOLS
    buf = (MERGE_STEPS, SEG, D)
    scratch = [
        pltpu.VMEM((2 * nc, D, WCOLS), BF16), pltpu.VMEM((nc, POOL_WIDTH, WCOLS), BF16),
        pltpu.VMEM((nc, LRU_WIDTH, WCOLS), BF16), pltpu.VMEM((nc, D, WCOLS), BF16),
        pltpu.VMEM((2,) + buf, F32), pltpu.SemaphoreType.DMA((2, SEG)),
        pltpu.VMEM((2,) + buf, F32), pltpu.SemaphoreType.DMA((2, SEG)),
        pltpu.VMEM((N_PB, TM, CB), BF16), pltpu.VMEM((N_LB, TM, CB), BF16),
        pltpu.VMEM((TM, D), BF16),
        pltpu.VMEM((2, WROWS, WCOLS), F32), pltpu.SemaphoreType.DMA((2,)),
    ]
    last_p = n_prompt - 1
    return pl.pallas_call(
        functools.partial(_merge_kernel, alpha, layer, n_prompt, n_tiles, S, T // (S * SEG), R, Ts),
        grid=(n_tiles,),
        in_specs=[
            _any_spec(), _any_spec(),
            pl.BlockSpec((N_PB, TM, CB), lambda i: (0, jnp.minimum(i, last_p), 0)),
            pl.BlockSpec((N_LB, TM, CB), lambda i: (0, jnp.minimum(i, last_p), 0)),
            pl.BlockSpec((N_PB, TM, CB), lambda i: (0, jnp.maximum(i - n_prompt, 0), 0)),
            pl.BlockSpec((N_LB, TM, CB), lambda i: (0, jnp.maximum(i - n_prompt, 0), 0)),
        ] + [_any_spec() for _ in range(4)] + [_vmem_spec() for _ in range(3)],
        out_specs=[_any_spec(), _any_spec()],
        out_shape=[jax.ShapeDtypeStruct(xp.shape, F32), jax.ShapeDtypeStruct(xs.shape, F32)],
        scratch_shapes=scratch,
        compiler_params=pltpu.CompilerParams(
            dimension_semantics=("arbitrary",), vmem_limit_bytes=VMEM_LIMIT_BYTES),
        name="merge",
    )(xp, xs, *yp, *ys, *weights_f32, *params)


def _block_diag_pairs(w):
    nb = w.shape[0]
    w2 = w.reshape(nb // 2, 2, LRU_BLOCK_DIM, LRU_BLOCK_DIM)
    z = jnp.zeros_like(w2[:, 0])
    top = jnp.concatenate([w2[:, 0], z], axis=2)
    bot = jnp.concatenate([z, w2[:, 1]], axis=2)
    return jnp.concatenate([top, bot], axis=1).astype(BF16)


def _layer_params(l, b_merge, pool_w, pool_scale, conv_w, conv_b, lru_wa, lru_ba, lru_wx, lru_bx, lru_L,
                  ln_g, ln_b):
    row = lambda v: v.reshape(1, -1)
    mixer = (
        pool_w[l].astype(BF16), row(pool_scale[l]), conv_w[l], row(conv_b[l]),
        _block_diag_pairs(lru_wa[l]), _block_diag_pairs(lru_wx[l]),
        row(lru_ba[l]), row(lru_bx[l]), row(lru_L[l]),
    )
    assert len(mixer) == N_MIXER_PARAMS
    merge = (b_merge[l], row(ln_g[l]), row(ln_b[l]))
    return mixer, merge


def kernel(x_prompt, x_sample, state_pool, state_conv, state_h, w_in, b_merge, pool_w, pool_scale, conv_w, conv_b, lru_wa, lru_ba, lru_wx, lru_bx, lru_L, w_proj_a, w_proj_b, w_out, ln_g, ln_b):
    depth = w_in.shape[0]
    alpha = (2.0 * depth) ** 0.25
    B = x_prompt.shape[0]
    Bs = x_sample.shape[0]

    xp, xs = x_prompt, x_sample.reshape(Bs // SEG, SEG, -1)
    pool_p, conv_p, h_p, pool_s, conv_s, h_s = [], [], [], [], [], []
    for l in range(depth):
        mp, gp = _layer_params(l, b_merge, pool_w, pool_scale, conv_w, conv_b, lru_wa, lru_ba,
                               lru_wx, lru_bx, lru_L, ln_g, ln_b)
        ya, yb, pst, cst, hst, wmix = _mixer_prompt(xp, w_in, l, mp)
        pool_p.append(pst.transpose(0, 2, 1, 3).reshape(B, POOL_BUF, POOL_WIDTH))
        conv_p.append(cst.transpose(0, 2, 1, 3).reshape(B, CONV_BUF, LRU_WIDTH))
        h_p.append(hst.transpose(0, 2, 1, 3).reshape(B, LRU_WIDTH))

        yas, ybs, pst, cst, hst = _mixer_sample(
            xs, state_pool[l].reshape(Bs, -1), state_conv[l].reshape(Bs, -1), state_h[l], wmix, mp,
            PAST_LEN)
        xp, xs = _merge(xp, xs, (ya, yb), (yas, ybs), (w_in, w_proj_a, w_proj_b, w_out), l, gp, alpha)
        pool_s.append(pst.reshape(state_pool.shape[1:]))
        conv_s.append(cst.reshape(state_conv.shape[1:]))
        h_s.append(hst)

    return (xp, xs.reshape(x_sample.shape),
            jnp.stack(pool_p), jnp.stack(conv_p), jnp.stack(h_p),
            jnp.stack(pool_s), jnp.stack(conv_s), jnp.stack(h_s))
```
